```python
import math
import jax
import jax.numpy as jnp
from jax import lax
import numpy as np


D_MODEL = 4096
BATCH = 2
SEQ = 8192
DEPTH = 2

N_MIXERS = 2
NORM_EPS = 1e-6

DA_HEADS = 16
DA_HEAD_DIM = 128
DA_V_DIM = 2 * DA_HEAD_DIM
Q_BLOCK = 128

RW_HEAD_SIZE = 64
RW_HEADS = D_MODEL // RW_HEAD_SIZE
RW_DECAY_LORA = max(32, int(round(1.8 * D_MODEL ** 0.5 / 32)) * 32)
RW_ICLR_LORA = max(32, int(round(1.8 * D_MODEL ** 0.5 / 32)) * 32)
RW_GATE_LORA = max(32, int(round(0.6 * D_MODEL ** 0.8 / 32)) * 32)
RW_LNX_EPS = RW_HEAD_SIZE * 1e-5

PEER_HEADS = 8
PEER_NKEYS = 128
PEER_EXPERTS = PEER_NKEYS ** 2
PEER_HALF = 128
PEER_QDIM = 2 * PEER_HALF
PEER_TOPK = 16
PEER_CHUNK = 64

kernel_name = 'hybrid_diffattn_rwkv7_peer_encoder'


def rms_norm(x, g, eps=NORM_EPS):
    xf = x.astype(jnp.float32)
    y = xf * lax.rsqrt(jnp.mean(xf * xf, axis=-1, keepdims=True) + eps)
    return (y * g.astype(jnp.float32)).astype(x.dtype)


def alibi_slopes(n_heads):
    return 2.0 ** (-8.0 * jnp.arange(1, n_heads + 1, dtype=jnp.float32) / n_heads)


def differential_attention(h, w_qkv, q_norm, k_norm, lam_p, sub_norm, w_o, lambda_init):
    B, T, _ = h.shape
    H, d = DA_HEADS, DA_HEAD_DIM
    q, k, v = jnp.split(h @ w_qkv, 3, axis=-1)
    q = rms_norm(q.reshape(B, T, H, 2, d), q_norm)
    k = rms_norm(k.reshape(B, T, H, 2, d), k_norm)
    v = v.reshape(B, T, H, DA_V_DIM)
    lp = lam_p.astype(jnp.float32)
    lam = jnp.exp(jnp.sum(lp[0] * lp[1])) - jnp.exp(jnp.sum(lp[2] * lp[3])) + lambda_init
    slopes = alibi_slopes(H)
    k_pos = jnp.arange(T, dtype=jnp.int32)
    scale = d ** -0.5
    n_blk = T // Q_BLOCK
    q_blocks = q.reshape(B, n_blk, Q_BLOCK, H, 2, d).transpose(1, 0, 2, 3, 4, 5)
    starts = jnp.arange(n_blk, dtype=jnp.int32) * Q_BLOCK

    def one_block(args):
        q_blk, start = args
        s = jnp.einsum('bqhcd,bkhcd->bhcqk', q_blk, k).astype(jnp.float32) * scale
        q_pos = start + jnp.arange(Q_BLOCK, dtype=jnp.int32)
        dist = jnp.abs(q_pos[:, None] - k_pos[None, :]).astype(jnp.float32)
        s = s - slopes[None, :, None, None, None] * dist
        p = jax.nn.softmax(s, axis=-1)
        a = p[:, :, 0] - lam * p[:, :, 1]
        return jnp.einsum('bhqk,bkhe->bqhe', a.astype(v.dtype), v)

    o = lax.map(one_block, (q_blocks, starts))
    o = o.transpose(1, 0, 2, 3, 4).reshape(B, T, H, DA_V_DIM)
    o = rms_norm(o, sub_norm) * (1.0 - lambda_init)
    return o.reshape(B, T, H * DA_V_DIM) @ w_o


def _time_major(z):
    b, t, _ = z.shape
    return z.reshape(b, t, RW_HEADS, RW_HEAD_SIZE).transpose(1, 0, 2, 3).astype(jnp.float32)


def _wkv_scan(r, w, k, v, a, b, reverse):
    _, bsz, n_h, n = r.shape

    def step(S, inp):
        r_t, w_t, k_t, v_t, a_t, b_t = inp
        sa = jnp.einsum('bhij,bhj->bhi', S, a_t)
        S = S * w_t[:, :, None, :] + sa[..., None] * b_t[:, :, None, :] + v_t[..., None] * k_t[:, :, None, :]
        return S, jnp.einsum('bhij,bhj->bhi', S, r_t)

    S0 = jnp.zeros((bsz, n_h, n, n), jnp.float32)
    _, y = lax.scan(step, S0, (r, w, k, v, a, b), reverse=reverse)
    return y


def rwkv7_bidirectional(h, mu, w_rkv, w0, w1, w2, a0, a1, a2, g1, g2, k_k, k_a, r_k, ln_w, ln_b, w_o):
    B, T, D = h.shape
    H, N = RW_HEADS, RW_HEAD_SIZE
    f32 = jnp.float32
    zero = jnp.zeros_like(h[:, :1])
    xx = 0.5 * (jnp.concatenate([zero, h[:, :-1]], axis=1) + jnp.concatenate([h[:, 1:], zero], axis=1)) - h
    r, k, v = jnp.einsum('sbtd,sde->sbte', h[None] + xx[None] * mu[:3, None, None, :], w_rkv)
    xw = h + xx * mu[3]
    xa = h + xx * mu[4]
    g = jax.nn.sigmoid((h + xx * mu[5]) @ g1) @ g2
    kk = (k * k_k).reshape(B, T, H, N).astype(f32)
    kk = (kk / jnp.maximum(jnp.linalg.norm(kk, axis=-1, keepdims=True), 1e-12)).reshape(B, T, D)
    k = k.astype(f32)
    r_tm = _time_major(r)
    v_tm = _time_major(v)
    neg_kk_tm = _time_major(-kk)
    y = jnp.zeros((T, B, H, N), f32)
    k_sum = jnp.zeros((B, T, D), f32)
    for z, reverse in ((0, False), (1, True)):
        w_log = -jax.nn.softplus(-(w0[z] + jnp.tanh(xw @ w1[z]) @ w2[z]).astype(f32)) - 0.5
        decay = jnp.exp(-jnp.exp(w_log))
        a = jax.nn.sigmoid((a0[z] + (xa @ a1[z]) @ a2[z]).astype(f32))
        k_dir = k * (1.0 + (a - 1.0) * k_a.astype(f32))
        y = y + _wkv_scan(r_tm, _time_major(decay), _time_major(k_dir), v_tm, neg_kk_tm,
                          _time_major(kk * a), reverse)
        k_sum = k_sum + k_dir
    y = y.transpose(1, 0, 2, 3)
    mean = jnp.mean(y, axis=-1, keepdims=True)
    var = jnp.mean(jnp.square(y - mean), axis=-1, keepdims=True)
    y = ((y - mean) * lax.rsqrt(var + RW_LNX_EPS)).reshape(B, T, D) * ln_w.astype(f32) + ln_b.astype(f32)
    r_h = r.astype(f32).reshape(B, T, H, N)
    bonus = jnp.sum(r_h * k_sum.reshape(B, T, H, N) * r_k.astype(f32), axis=-1, keepdims=True) \
        * v.astype(f32).reshape(B, T, H, N)
    o = (y + bonus.reshape(B, T, D)) * g.astype(f32)
    return o.astype(h.dtype) @ w_o


def peer_ffn(h, w_q, sub_keys, u_tab, v_tab):
    B, T, D = h.shape
    n = B * T
    PH, K = PEER_HEADS, PEER_TOPK
    hf = h.reshape(n, D)
    q = (hf @ w_q).reshape(n, PH, 2, PEER_HALF)
    s = jnp.einsum('nphd,phkd->nphk', q, sub_keys).astype(jnp.float32)
    s1, i1 = lax.top_k(s[:, :, 0], K)
    s2, i2 = lax.top_k(s[:, :, 1], K)
    cand = (s1[..., :, None] + s2[..., None, :]).reshape(n, PH, K * K)
    cand_idx = (i1[..., :, None] * PEER_NKEYS + i2[..., None, :]).reshape(n, PH, K * K)
    top_s, j = lax.top_k(cand, K)
    idx = jnp.take_along_axis(cand_idx, j, axis=-1)
    gate = jax.nn.softmax(top_s, axis=-1).astype(h.dtype)
    n_chunk = n // PEER_CHUNK

    def one_chunk(args):
        hc, ic, gc = args
        act = jax.nn.gelu(jnp.einsum('cd,cpkd->cpk', hc, u_tab[ic]).astype(jnp.float32), approximate=False)
        return jnp.einsum('cpk,cpkd->cd', (gc * act.astype(gc.dtype)), v_tab[ic])

    out = lax.map(one_chunk, (hf.reshape(n_chunk, PEER_CHUNK, D),
                              idx.reshape(n_chunk, PEER_CHUNK, PH, K),
                              gate.reshape(n_chunk, PEER_CHUNK, PH, K)))
    return out.reshape(B, T, D)


def setup_inputs(seed: int = 0) -> dict:
    key = jax.random.key(seed)
    ks = jax.random.split(key, 40)
    ctr = [0]

    def next_key():
        k = ks[ctr[0]]
        ctr[0] += 1
        return k

    def nrm(shape, scale):
        return jax.random.normal(next_key(), shape, jnp.float32) * scale

    D = D_MODEL
    NA = (DEPTH + 1) // 2
    NR = DEPTH // 2
    d = DA_HEAD_DIM
    return {
        'x': nrm((BATCH, SEQ, D), 1.0),
        'norm_mix': 1.0 + nrm((DEPTH, D), 0.02),
        'norm_ffn': 1.0 + nrm((DEPTH, D), 0.02),
        'attn_w_qkv': nrm((NA, D, 3 * D), D ** -0.5),
        'attn_q_norm': 1.0 + nrm((NA, d), 0.02),
        'attn_k_norm': 1.0 + nrm((NA, d), 0.02),
        'attn_lambda': nrm((NA, 4, d), 0.1),
        'attn_sub_norm': 1.0 + nrm((NA, DA_V_DIM), 0.02),
        'attn_w_o': nrm((NA, DA_HEADS * DA_V_DIM, D), D ** -0.5),
        'rwkv_mu': jax.random.uniform(next_key(), (NR, 6, D), jnp.float32),
        'rwkv_w_rkv': nrm((NR, 3, D, D), D ** -0.5),
        'rwkv_w0': -1.0 + nrm((NR, 2, D), 0.5),
        'rwkv_w1': nrm((NR, 2, D, RW_DECAY_LORA), D ** -0.5),
        'rwkv_w2': nrm((NR, 2, RW_DECAY_LORA, D), 0.3 * RW_DECAY_LORA ** -0.5),
        'rwkv_a0': nrm((NR, 2, D), 0.3),
        'rwkv_a1': nrm((NR, 2, D, RW_ICLR_LORA), D ** -0.5),
        'rwkv_a2': nrm((NR, 2, RW_ICLR_LORA, D), 0.3 * RW_ICLR_LORA ** -0.5),
        'rwkv_g1': nrm((NR, D, RW_GATE_LORA), D ** -0.5),
        'rwkv_g2': nrm((NR, RW_GATE_LORA, D), RW_GATE_LORA ** -0.5),
        'rwkv_k_k': 1.0 + nrm((NR, D), 0.1),
        'rwkv_k_a': 1.0 + nrm((NR, D), 0.1),
        'rwkv_r_k': nrm((NR, RW_HEADS, RW_HEAD_SIZE), 0.1),
        'rwkv_ln_w': 1.0 + nrm((NR, D), 0.02),
        'rwkv_ln_b': nrm((NR, D), 0.01),
        'rwkv_w_o': nrm((NR, D, D), D ** -0.5),
        'peer_w_q': nrm((DEPTH, D, PEER_HEADS * PEER_QDIM), D ** -0.5),
        'peer_sub_keys': nrm((DEPTH, PEER_HEADS, 2, PEER_NKEYS, PEER_HALF), PEER_HALF ** -0.5),
        'peer_u': nrm((DEPTH, PEER_EXPERTS, D), D ** -0.5),
        'peer_v': nrm((DEPTH, PEER_EXPERTS, D), PEER_HEADS ** -0.5),
    }


def reference(x, norm_mix, norm_ffn, attn_w_qkv, attn_q_norm, attn_k_norm, attn_lambda, attn_sub_norm,
              attn_w_o, rwkv_mu, rwkv_w_rkv, rwkv_w0, rwkv_w1, rwkv_w2, rwkv_a0, rwkv_a1, rwkv_a2, rwkv_g1,
              rwkv_g2, rwkv_k_k, rwkv_k_a, rwkv_r_k, rwkv_ln_w, rwkv_ln_b, rwkv_w_o, peer_w_q, peer_sub_keys,
              peer_u, peer_v):
    for i in range(DEPTH):
        j = i // N_MIXERS
        h = rms_norm(x, norm_mix[i])
        if i % N_MIXERS == 0:
            lambda_init = 0.8 - 0.6 * math.exp(-0.3 * i)
            mix = differential_attention(h, attn_w_qkv[j], attn_q_norm[j], attn_k_norm[j], attn_lambda[j],
                                         attn_sub_norm[j], attn_w_o[j], lambda_init)
        else:
            mix = rwkv7_bidirectional(h, rwkv_mu[j], rwkv_w_rkv[j], rwkv_w0[j], rwkv_w1[j], rwkv_w2[j],
                                      rwkv_a0[j], rwkv_a1[j], rwkv_a2[j], rwkv_g1[j], rwkv_g2[j],
                                      rwkv_k_k[j], rwkv_k_a[j], rwkv_r_k[j], rwkv_ln_w[j], rwkv_ln_b[j],
                                      rwkv_w_o[j])
        x = x + mix
        x = x + peer_ffn(rms_norm(x, norm_ffn[i]), peer_w_q[i], peer_sub_keys[i], peer_u[i], peer_v[i])
    return x
```

```python
import functools
import math

import jax
import jax.numpy as jnp
from jax import lax
from jax.experimental import pallas as pl
from jax.experimental.pallas import tpu as pltpu

NORM_EPS = 1e-6
N_MIXERS = 2

DA_HEAD_DIM = 128
DA_V_DIM = 2 * DA_HEAD_DIM

RW_HEAD_SIZE = 64
RW_LNX_EPS = RW_HEAD_SIZE * 1e-5
RW_CHUNK = 64

PEER_HEADS = 8
PEER_NKEYS = 128
PEER_HALF = 128
PEER_TOPK = 16

V7X_VMEM_LIMIT_BYTES = 56 * 1024 * 1024
LANES = 128

_NT = (((1,), (1,)), ((), ()))
_TN = (((0,), (0,)), ((), ()))

_NEG_INF = float("-inf")


def _params(*sem):
    return pltpu.CompilerParams(dimension_semantics=sem,
                                vmem_limit_bytes=V7X_VMEM_LIMIT_BYTES)


def _pick(n, pref):
    if n <= pref:
        return n
    b = pref
    while n % b:
        b //= 2
    return b


def _rmsnorm_kernel(x_ref, g_ref, o_ref):
    x = x_ref[...]
    ms = jnp.mean(x * x, axis=-1, keepdims=True)
    o_ref[...] = (x * lax.rsqrt(ms + NORM_EPS) * g_ref[...]).astype(o_ref.dtype)


def rmsnorm_bf16(x, g):
    n, d = x.shape
    tm = _pick(n, 256)
    return pl.pallas_call(
        _rmsnorm_kernel,
        grid=(n // tm,),
        in_specs=[pl.BlockSpec((tm, d), lambda i: (i, 0)),
                  pl.BlockSpec((1, d), lambda i: (0, 0))],
        out_specs=pl.BlockSpec((tm, d), lambda i: (i, 0)),
        out_shape=jax.ShapeDtypeStruct((n, d), jnp.bfloat16),
        compiler_params=_params("parallel"),
        name="rmsnorm",
    )(x, g.reshape(1, d))


def _mm_kernel(*refs, epilogue, n_norm_tiles):
    a_ref, b_ref = refs[0], refs[1]
    o_ref = refs[-1]
    acc = jnp.dot(a_ref[...], b_ref[...], preferred_element_type=jnp.float32)
    if epilogue == "plain":
        o_ref[...] = acc.astype(o_ref.dtype)
    elif epilogue == "residual":
        o_ref[...] = refs[2][...] + acc
    elif epilogue == "tanh":
        o_ref[...] = jnp.tanh(acc).astype(o_ref.dtype)
    elif epilogue == "sigmoid":
        o_ref[...] = jax.nn.sigmoid(acc).astype(o_ref.dtype)
    elif epilogue == "headnorm":
        gain_ref = refs[2]
        j = pl.program_id(1)

        @pl.when(j < n_norm_tiles)
        def _():
            for c in range(acc.shape[1] // DA_HEAD_DIM):
                sl = slice(c * DA_HEAD_DIM, (c + 1) * DA_HEAD_DIM)
                blk = acc[:, sl]
                ms = jnp.mean(blk * blk, axis=-1, keepdims=True)
                o_ref[:, sl] = (blk * lax.rsqrt(ms + NORM_EPS) * gain_ref[:, sl]).astype(o_ref.dtype)

        @pl.when(j >= n_norm_tiles)
        def _():
            o_ref[...] = acc.astype(o_ref.dtype)
    else:
        raise ValueError(epilogue)


def matmul(a, b, *, epilogue="plain", out_dtype=jnp.float32, extra=None,
           n_norm_cols=0, tm=1024, tn=512, name="matmul"):
    m, k = a.shape
    k2, n = b.shape
    assert k == k2
    tm = _pick(m, tm)
    tn = _pick(n, tn)
    in_specs = [pl.BlockSpec((tm, k), lambda i, j: (i, 0)),
                pl.BlockSpec((k, tn), lambda i, j: (0, j))]
    args = [a, b]
    if epilogue == "residual":
        in_specs.append(pl.BlockSpec((tm, tn), lambda i, j: (i, j)))
        args.append(extra)
    elif epilogue == "headnorm":
        in_specs.append(pl.BlockSpec((1, tn), lambda i, j: (0, j)))
        args.append(extra.reshape(1, n))
        assert n_norm_cols % tn == 0
    kern = functools.partial(_mm_kernel, epilogue=epilogue,
                             n_norm_tiles=n_norm_cols // tn)
    return pl.pallas_call(
        kern,
        grid=(m // tm, n // tn),
        in_specs=in_specs,
        out_specs=pl.BlockSpec((tm, tn), lambda i, j: (i, j)),
        out_shape=jax.ShapeDtypeStruct((m, n), out_dtype),
        compiler_params=_params("parallel", "parallel"),
        name=name,
    )(*args)


def _attn_kernel(par_ref, q_ref, k_ref, v_ref, sub_ref, o_ref, m_scr, l_scr, acc_scr,
                 *, tq, tk, tkb, n_heads, out_scale):
    h = pl.program_id(1)
    qi = pl.program_id(2)
    kj = pl.program_id(3)
    d = DA_HEAD_DIM

    @pl.when(kj == 0)
    def _():
        m_scr[...] = jnp.full(m_scr.shape, _NEG_INF, jnp.float32)
        l_scr[...] = jnp.zeros(l_scr.shape, jnp.float32)
        acc_scr[...] = jnp.zeros(acc_scr.shape, jnp.float32)

    slope = par_ref[1 + h]
    rel = (lax.broadcasted_iota(jnp.int32, (tq, tk), 0)
           - lax.broadcasted_iota(jnp.int32, (tq, tk), 1))
    q = q_ref[...]
    for kb in range(tkb // tk):
        off = qi * tq - (kj * tkb + kb * tk)
        bias = jnp.abs(rel + off).astype(jnp.float32) * (-slope)
        kblk = k_ref[kb * tk:(kb + 1) * tk, :]
        vblk = v_ref[kb * tk:(kb + 1) * tk, :]
        for c in range(2):
            s = lax.dot_general(q[:, c * d:(c + 1) * d], kblk[:, c * d:(c + 1) * d], _NT,
                                preferred_element_type=jnp.float32) + bias
            m_prev = m_scr[c]
            m_new = jnp.maximum(m_prev, jnp.max(s, axis=-1, keepdims=True))
            alpha = jnp.exp(m_prev - m_new)
            p = jnp.exp(s - m_new[:, :1])
            l_scr[c] = alpha * l_scr[c] + jnp.sum(p, axis=-1, keepdims=True)
            acc_scr[c] = acc_scr[c] * alpha[:, :1] + jnp.dot(
                p.astype(jnp.bfloat16), vblk, preferred_element_type=jnp.float32)
            m_scr[c] = m_new

    @pl.when(kj == pl.num_programs(3) - 1)
    def _():
        lam = par_ref[0]
        o = acc_scr[0] / l_scr[0][:, :1] - lam * (acc_scr[1] / l_scr[1][:, :1])
        ms = jnp.mean(o * o, axis=-1, keepdims=True)
        o_ref[...] = (o * lax.rsqrt(ms + NORM_EPS) * (sub_ref[...] * out_scale)).astype(o_ref.dtype)


def diff_attention_core(qkv, par, sub_norm, *, batch, seq, n_heads, out_scale,
                        tq=512, tk=512, tkb=2048):
    n = batch * seq
    tq = _pick(seq, tq)
    tkb = _pick(seq, tkb)
    tk = _pick(tkb, tk)
    nq, nk = seq // tq, seq // tkb
    vd = DA_V_DIM
    kern = functools.partial(_attn_kernel, tq=tq, tk=tk, tkb=tkb, n_heads=n_heads,
                             out_scale=out_scale)
    return pl.pallas_call(
        kern,
        grid=(batch, n_heads, nq, nk),
        in_specs=[
            pl.BlockSpec(memory_space=pltpu.SMEM),
            pl.BlockSpec((tq, vd), lambda b, h, i, j: (b * nq + i, h)),
            pl.BlockSpec((tkb, vd), lambda b, h, i, j: (b * nk + j, n_heads + h)),
            pl.BlockSpec((tkb, vd), lambda b, h, i, j: (b * nk + j, 2 * n_heads + h)),
            pl.BlockSpec((1, vd), lambda b, h, i, j: (0, 0)),
        ],
        out_specs=pl.BlockSpec((tq, vd), lambda b, h, i, j: (b * nq + i, h)),
        out_shape=jax.ShapeDtypeStruct((n, n_heads * vd), jnp.bfloat16),
        scratch_shapes=[pltpu.VMEM((2, tq, LANES), jnp.float32),
                        pltpu.VMEM((2, tq, LANES), jnp.float32),
                        pltpu.VMEM((2, tq, vd), jnp.float32)],
        compiler_params=_params("parallel", "parallel", "parallel", "arbitrary"),
        name="diff_attn",
    )(par, qkv, qkv, qkv, sub_norm.reshape(1, vd))


def differential_attention(x2, g_mix, w_qkv, q_norm, k_norm, lam_p, sub_norm, w_o,
                           lambda_init, *, batch, seq):
    n, dm = x2.shape
    d = DA_HEAD_DIM
    n_heads = w_qkv.shape[1] // (3 * 2 * d)
    h = rmsnorm_bf16(x2, g_mix)
    qk_cols = 2 * n_heads * 2 * d
    gains = jnp.concatenate([jnp.tile(q_norm * (d ** -0.5), 2 * n_heads),
                             jnp.tile(k_norm, 2 * n_heads),
                             jnp.ones((n_heads * DA_V_DIM,), jnp.float32)])
    qkv = matmul(h, w_qkv.astype(jnp.bfloat16), epilogue="headnorm", extra=gains,
                 n_norm_cols=qk_cols, out_dtype=jnp.bfloat16, name="attn_qkv")
    lp = lam_p.astype(jnp.float32)
    lam = jnp.exp(jnp.sum(lp[0] * lp[1])) - jnp.exp(jnp.sum(lp[2] * lp[3])) + lambda_init
    slopes = 2.0 ** (-8.0 * jnp.arange(1, n_heads + 1, dtype=jnp.float32) / n_heads)
    par = jnp.concatenate([lam.reshape(1), slopes]).astype(jnp.float32)
    o = diff_attention_core(qkv, par, sub_norm, batch=batch, seq=seq, n_heads=n_heads,
                            out_scale=1.0 - lambda_init)
    return matmul(o, w_o.astype(jnp.bfloat16), epilogue="residual", extra=x2, name="attn_out")


def _top_values(s, k):
    rows = lax.broadcasted_iota(jnp.int32, s.shape, 0)
    n_rows = s.shape[0]
    vals = []
    for _ in range(k):
        m = jnp.max(s, axis=0, keepdims=True)
        vals.append(m)
        first = jnp.min(jnp.where(s == m, rows, n_rows), axis=0, keepdims=True)
        s = jnp.where(rows == first, _NEG_INF, s)
    return vals


def _peer_route_kernel(q_ref, keys_ref, s1_ref, e1_ref, s2_ref, e2_ref, tau_ref):
    kk = PEER_TOPK
    half = PEER_HALF
    taus = []
    for p in range(PEER_HEADS):
        qa = q_ref[:, (2 * p) * half:(2 * p + 1) * half]
        qb = q_ref[:, (2 * p + 1) * half:(2 * p + 2) * half]
        s1 = lax.dot_general(keys_ref[p, 0], qa, _NT, preferred_element_type=jnp.float32,
                             precision=lax.Precision.HIGHEST)
        s2 = lax.dot_general(keys_ref[p, 1], qb, _NT, preferred_element_type=jnp.float32,
                             precision=lax.Precision.HIGHEST)
        a = _top_values(s1, kk)
        b = _top_values(s2, kk)
        cands = [a[i] + b[j] for i in range(kk) for j in range(kk) if (i + 1) * (j + 1) <= kk]
        pad = (-len(cands)) % 8
        cands += [jnp.full_like(a[0], _NEG_INF)] * pad
        top = _top_values(jnp.concatenate(cands, axis=0), kk)
        mx = top[0]
        z = jnp.zeros_like(mx)
        for t in top:
            z = z + jnp.exp(t - mx)
        taus.append(top[kk - 1])
        s1_ref[p] = s1
        s2_ref[p] = s2
        e1_ref[p] = jnp.exp(s1 - a[0])
        e2_ref[p] = jnp.exp(s2 - b[0]) / z
    tau_ref[...] = jnp.concatenate(taus, axis=0)


def peer_route(q, sub_keys, *, tn=256):
    n = q.shape[0]
    tn = _pick(n, tn)
    ph, nk = PEER_HEADS, PEER_NKEYS
    big = jax.ShapeDtypeStruct((ph, nk, n), jnp.float32)
    bspec = pl.BlockSpec((ph, nk, tn), lambda i: (0, 0, i))
    return pl.pallas_call(
        _peer_route_kernel,
        grid=(n // tn,),
        in_specs=[pl.BlockSpec((tn, q.shape[1]), lambda i: (i, 0)),
                  pl.BlockSpec(sub_keys.shape, lambda i: (0, 0, 0, 0))],
        out_specs=[bspec, bspec, bspec, bspec, pl.BlockSpec((ph, tn), lambda i: (0, i))],
        out_shape=[big, big, big, big, jax.ShapeDtypeStruct((ph, n), jnp.float32)],
        compiler_params=_params("parallel"),
        name="peer_route",
    )(q, sub_keys)


def _peer_dense_kernel(h_ref, u_ref, v_ref, x_hbm, s1_ref, e1_ref, s2_ref, e2_ref, tau_ref,
                       o_ref, x_sem, *, a_per_tile, a_rows):
    i = pl.program_id(0)
    e = pl.program_id(1)
    tn = o_ref.shape[0]

    @pl.when(e == 0)
    def _():
        cp = pltpu.make_async_copy(x_hbm.at[pl.ds(pl.multiple_of(i * tn, tn), tn), :], o_ref, x_sem)
        cp.start()
        cp.wait()

    act = lax.dot_general(u_ref[...], h_ref[...], _NT, preferred_element_type=jnp.float32)
    act = 0.5 * act * (1.0 + lax.erf(act * (2.0 ** -0.5)))
    nk = PEER_NKEYS
    a_base = lax.rem(e * a_per_tile, a_rows)
    parts = []
    for ai in range(a_per_tile):
        a = a_base + ai
        w = None
        for p in range(PEER_HEADS):
            s1row = s1_ref[p, pl.ds(a, 1), :]
            e1row = e1_ref[p, pl.ds(a, 1), :]
            taurow = tau_ref[pl.ds(p, 1), :]
            sel = jnp.where(s1row + s2_ref[p] >= taurow, e2_ref[p], 0.0) * e1row
            w = sel if w is None else w + sel
        parts.append(act[ai * nk:(ai + 1) * nk, :] * w)
    pt = jnp.concatenate(parts, axis=0)
    o_ref[...] += jnp.dot(pt.T.astype(jnp.bfloat16), v_ref[...], preferred_element_type=jnp.float32)


def peer_dense(hf, u, v, x2, s1, e1, s2, e2, tau, *, tn=512, te=512):
    n, d = hf.shape
    n_exp = u.shape[0]
    tn = _pick(n, tn)
    te = _pick(n_exp, te)
    nk, ph = PEER_NKEYS, PEER_HEADS
    a_per_tile = te // nk
    a_rows = max(a_per_tile, 8)
    assert a_rows % a_per_tile == 0
    kern = functools.partial(_peer_dense_kernel, a_per_tile=a_per_tile, a_rows=a_rows)
    tok = lambda i, e: (i, 0)
    once = pl.Buffered(1)
    first = pl.BlockSpec((ph, a_rows, tn), lambda i, e: (0, (e * a_per_tile) // a_rows, i))
    second = pl.BlockSpec((ph, nk, tn), lambda i, e: (0, 0, i), pipeline_mode=once)
    return pl.pallas_call(
        kern,
        grid=(n // tn, n_exp // te),
        in_specs=[pl.BlockSpec((tn, d), tok, pipeline_mode=once),
                  pl.BlockSpec((te, d), lambda i, e: (e, 0)),
                  pl.BlockSpec((te, d), lambda i, e: (e, 0)),
                  pl.BlockSpec(memory_space=pl.ANY),
                  first, first, second, second,
                  pl.BlockSpec((ph, tn), lambda i, e: (0, i))],
        out_specs=pl.BlockSpec((tn, d), tok),
        out_shape=jax.ShapeDtypeStruct((n, d), jnp.float32),
        scratch_shapes=[pltpu.SemaphoreType.DMA(())],
        compiler_params=_params("parallel", "arbitrary"),
        name="peer_dense",
    )(hf, u, v, x2, s1, e1, s2, e2, tau)


def peer_layer(x2, g_ffn, w_q, sub_keys, u_tab, v_tab):
    hf = rmsnorm_bf16(x2, g_ffn)
    q = matmul(hf, w_q.astype(jnp.bfloat16), name="peer_q")
    s1, e1, s2, e2, tau = peer_route(q, sub_keys)
    return peer_dense(hf, u_tab.astype(jnp.bfloat16), v_tab.astype(jnp.bfloat16), x2,
                      s1, e1, s2, e2, tau)


def _rwkv_mix_kernel(x_ref, xp_ref, xn_ref, g_ref, mu_ref, *o_refs, tm, seq):
    i = pl.program_id(0)
    g = g_ref[...]

    def norm(x):
        ms = jnp.mean(x * x, axis=-1, keepdims=True)
        return x * lax.rsqrt(ms + NORM_EPS) * g

    h = norm(x_ref[...])
    hp = norm(xp_ref[...])[7:8, :]
    hn = norm(xn_ref[...])[0:1, :]
    t0 = lax.rem(i * tm, seq)
    prev_row = jnp.where(t0 == 0, 0.0, hp)
    next_row = jnp.where(t0 + tm == seq, 0.0, hn)
    rows = lax.broadcasted_iota(jnp.int32, h.shape, 0)
    h_prev = jnp.where(rows == 0, prev_row, pltpu.roll(h, 1, axis=0))
    h_next = jnp.where(rows == tm - 1, next_row, pltpu.roll(h, tm - 1, axis=0))
    xx = 0.5 * (h_prev + h_next) - h
    for idx, o_ref in enumerate(o_refs):
        o_ref[...] = (h + xx * mu_ref[idx:idx + 1, :]).astype(o_ref.dtype)


def rwkv_mix(x2, g_mix, mu, *, seq, tm=256):
    n, d = x2.shape
    tm = _pick(seq, tm)
    assert tm % 8 == 0 and seq % tm == 0
    r8 = tm // 8
    last = n // 8 - 1
    kern = functools.partial(_rwkv_mix_kernel, tm=tm, seq=seq)
    out = jax.ShapeDtypeStruct((n, d), jnp.bfloat16)
    return pl.pallas_call(
        kern,
        grid=(n // tm,),
        in_specs=[pl.BlockSpec((tm, d), lambda i: (i, 0)),
                  pl.BlockSpec((8, d), lambda i: (jnp.maximum(i * r8 - 1, 0), 0)),
                  pl.BlockSpec((8, d), lambda i: (jnp.minimum((i + 1) * r8, last), 0)),
                  pl.BlockSpec((1, d), lambda i: (0, 0)),
                  pl.BlockSpec((8, d), lambda i: (0, 0))],
        out_specs=[pl.BlockSpec((tm, d), lambda i: (i, 0))] * 6,
        out_shape=[out] * 6,
        compiler_params=_params("parallel"),
        name="rwkv_mix",
    )(x2, x2, x2, g_mix.reshape(1, d), jnp.pad(mu, ((0, 2), (0, 0))))


def _head_ones(width, scale):
    r = lax.broadcasted_iota(jnp.int32, (width, width), 0) // RW_HEAD_SIZE
    c = lax.broadcasted_iota(jnp.int32, (width, width), 1) // RW_HEAD_SIZE
    return jnp.where(r == c, scale, 0.0).astype(jnp.float32)


def _head_sum(x, ones):
    return jnp.dot(x, ones, preferred_element_type=jnp.float32,
                   precision=lax.Precision.HIGHEST)


def _rwkv_prep_kernel(k_ref, wl_ref, al_ref, gl_ref, w2_ref, a2_ref, g2_ref, w0_ref, a0_ref,
                      kkw_ref, lw_ref, as_ref, kk_ref, g_ref, *, lora):
    for z in range(2):
        sl = slice(z * lora, (z + 1) * lora)
        wz = w0_ref[z:z + 1, :] + jnp.dot(wl_ref[:, sl], w2_ref[z], preferred_element_type=jnp.float32)
        y = -wz
        softplus = jnp.maximum(y, 0.0) + jnp.log1p(jnp.exp(-jnp.abs(y)))
        lw_ref[z] = -jnp.exp(-softplus - 0.5)
        az = a0_ref[z:z + 1, :] + jnp.dot(al_ref[:, sl], a2_ref[z], preferred_element_type=jnp.float32)
        as_ref[z] = jax.nn.sigmoid(az)
    g_ref[...] = jnp.dot(gl_ref[...], g2_ref[...], preferred_element_type=jnp.float32)
    kr = k_ref[...] * kkw_ref[...]
    ss = _head_sum(kr * kr, _head_ones(kr.shape[1], 1.0))
    kk_ref[...] = kr / jnp.maximum(jnp.sqrt(ss), 1e-12)


def rwkv_prep(k, wl, al, gl, w2, a2, g2, w0, a0, k_k, *, tm=512, tc=512):
    n, d = k.shape
    tm = _pick(n, tm)
    tc = _pick(d, tc)
    lora = w2.shape[1]
    glw = gl.shape[1]
    kern = functools.partial(_rwkv_prep_kernel, lora=lora)
    row = lambda i, j: (i, 0)
    tile = lambda i, j: (i, j)
    two = jax.ShapeDtypeStruct((2, n, d), jnp.float32)
    one = jax.ShapeDtypeStruct((n, d), jnp.float32)
    return pl.pallas_call(
        kern,
        grid=(n // tm, d // tc),
        in_specs=[pl.BlockSpec((tm, tc), tile),
                  pl.BlockSpec((tm, 2 * lora), row),
                  pl.BlockSpec((tm, 2 * lora), row),
                  pl.BlockSpec((tm, glw), row),
                  pl.BlockSpec((2, lora, tc), lambda i, j: (0, 0, j)),
                  pl.BlockSpec((2, lora, tc), lambda i, j: (0, 0, j)),
                  pl.BlockSpec((glw, tc), lambda i, j: (0, j)),
                  pl.BlockSpec((2, tc), lambda i, j: (0, j)),
                  pl.BlockSpec((2, tc), lambda i, j: (0, j)),
                  pl.BlockSpec((1, tc), lambda i, j: (0, j))],
        out_specs=[pl.BlockSpec((2, tm, tc), lambda i, j: (0, i, j)),
                   pl.BlockSpec((2, tm, tc), lambda i, j: (0, i, j)),
                   pl.BlockSpec((tm, tc), tile),
                   pl.BlockSpec((tm, tc), tile)],
        out_shape=[two, two, one, one],
        compiler_params=_params("parallel", "parallel"),
        name="rwkv_prep",
    )(k, wl, al, gl, w2, a2, g2, w0, a0, k_k.reshape(1, d))


def _split3(x):
    hi = x.astype(jnp.bfloat16)
    r1 = x - hi.astype(jnp.float32)
    mid = r1.astype(jnp.bfloat16)
    lo = (r1 - mid.astype(jnp.float32)).astype(jnp.bfloat16)
    return hi, mid, lo


def _bdot(a, b):
    return jnp.dot(a.astype(jnp.bfloat16), b.astype(jnp.bfloat16),
                   preferred_element_type=jnp.float32)


def _rwkv_scan_kernel(r_ref, k_ref, v_ref, kk_ref, lw_ref, as_ref, ka_ref, y_ref, s_scr,
                      *, n_chunks, n_heads):
    z = pl.program_id(0)
    c = pl.program_id(3)
    L = RW_CHUNK
    N = RW_HEAD_SIZE
    bf = jnp.bfloat16

    @pl.when(c == 0)
    def _():
        s_scr[...] = jnp.zeros(s_scr.shape, jnp.float32)

    sgn = 1 - 2 * z
    row = lax.broadcasted_iota(jnp.int32, (L, 2 * L), 0)
    col = lax.broadcasted_iota(jnp.int32, (L, 2 * L), 1)
    col = jnp.where(col >= L, col - L, col)
    delta = (col - row) * sgn
    strict2 = delta < 0
    incl2 = delta <= 0
    tri = jnp.where(incl2[:, :L], 1.0, 0.0).astype(bf)
    ka = ka_ref[...]

    def chunk(ci, carry):
        cc = jnp.where(z == 0, ci, n_chunks - 1 - ci)
        rows = pl.ds(pl.multiple_of(cc * L, L), L)
        lw = lw_ref[rows, :]
        a_s = as_ref[rows, :]
        r = r_ref[rows, :]
        k = k_ref[rows, :]
        v = v_ref[rows, :]
        kk = kk_ref[rows, :]
        hi, mid, lo = _split3(lw)
        cw = (jnp.dot(tri, hi, preferred_element_type=jnp.float32)
              + jnp.dot(tri, mid, preferred_element_type=jnp.float32)
              + jnp.dot(tri, lo, preferred_element_type=jnp.float32))
        tot = jnp.sum(lw, axis=0, keepdims=True)
        w_inv = jnp.exp(-cw)
        w_rem = jnp.exp(tot - cw)
        kd = k * (1.0 + (a_s - 1.0) * ka)
        ba = kk * a_s
        rt = r * jnp.exp(cw)
        at = -kk * jnp.exp(cw - lw)
        bt = ba * w_inv
        kt = kd * w_inv
        bp = ba * w_rem
        kp = kd * w_rem
        w_tot = jnp.exp(tot)
        for j in range(n_heads):
            sl = slice(j * N, (j + 1) * N)
            g_mat = jnp.concatenate([at[:, sl], rt[:, sl]], axis=0).astype(bf)
            bk = jnp.concatenate([bt[:, sl], kt[:, sl]], axis=0).astype(bf)
            bkp = jnp.concatenate([bp[:, sl], kp[:, sl]], axis=0).astype(bf)
            s_old = s_scr[j]
            m1 = lax.dot_general(g_mat, bk, _NT, preferred_element_type=jnp.float32)
            m2 = lax.dot_general(g_mat, s_old.astype(bf), _NT, preferred_element_type=jnp.float32)
            a_all = jnp.where(strict2, m1[:L, :], 0.0)
            r_all = jnp.where(incl2, m1[L:, :], 0.0)
            vj = v[:, sl]
            x = m2[:L, :] + _bdot(a_all[:, L:], vj)
            p = a_all[:, :L]
            n_stage = L.bit_length() - 1
            for st in range(n_stage):
                x = x + _bdot(p, x)
                if st < n_stage - 1:
                    p = _bdot(p, p)
            uv = jnp.concatenate([x, vj], axis=0).astype(bf)
            y_ref[rows, sl] = m2[L:, :] + jnp.dot(r_all.astype(bf), uv,
                                                  preferred_element_type=jnp.float32)
            s_scr[j] = s_old * w_tot[:, sl] + lax.dot_general(
                uv, bkp, _TN, preferred_element_type=jnp.float32)
        return carry

    lax.fori_loop(0, n_chunks, chunk, 0)


def rwkv_scan(r, k, v, kk, lw, asig, k_a, *, batch, seq, heads_per_step=4, rows_per_step=256):
    d = r.shape[-1]
    N, L = RW_HEAD_SIZE, RW_CHUNK
    hw = _pick(d, heads_per_step * N)
    tb = _pick(seq, rows_per_step)
    assert tb % L == 0
    nt = seq // tb
    kern = functools.partial(_rwkv_scan_kernel, n_chunks=tb // L, n_heads=hw // N)

    def tmap(z, c):
        return c + z * (nt - 1 - 2 * c)

    shared = pl.BlockSpec((None, tb, hw), lambda z, b, g, c: (b, tmap(z, c), g))
    per_dir = pl.BlockSpec((None, None, tb, hw), lambda z, b, g, c: (z, b, tmap(z, c), g))
    return pl.pallas_call(
        kern,
        grid=(2, batch, d // hw, nt),
        in_specs=[shared, shared, shared, shared, per_dir, per_dir,
                  pl.BlockSpec((1, hw), lambda z, b, g, c: (0, g))],
        out_specs=per_dir,
        out_shape=jax.ShapeDtypeStruct((2, batch, seq, d), jnp.float32),
        scratch_shapes=[pltpu.VMEM((hw // N, N, N), jnp.float32)],
        compiler_params=_params("parallel", "parallel", "parallel", "arbitrary"),
        name="rwkv_scan",
    )(r, k, v, kk, lw, asig, k_a.reshape(1, d))


def _rwkv_post_kernel(y_ref, r_ref, k_ref, v_ref, as_ref, g_ref, lnw_ref, lnb_ref, ka_ref, rk_ref,
                      o_ref):
    y = y_ref[0] + y_ref[1]
    width = y.shape[1]
    avg = _head_ones(width, 1.0 / RW_HEAD_SIZE)
    mean = _head_sum(y, avg)
    yc = y - mean
    var = _head_sum(yc * yc, avg)
    yn = yc * lax.rsqrt(var + RW_LNX_EPS) * lnw_ref[...] + lnb_ref[...]
    k_sum = k_ref[...] * (2.0 + (as_ref[0] + as_ref[1] - 2.0) * ka_ref[...])
    bonus = _head_sum(r_ref[...] * k_sum * rk_ref[...], _head_ones(width, 1.0)) * v_ref[...]
    o_ref[...] = ((yn + bonus) * g_ref[...]).astype(o_ref.dtype)


def rwkv_post(y, r, k, v, asig, g, ln_w, ln_b, k_a, r_k, *, tm=256, tc=512):
    n, d = r.shape
    tm = _pick(n, tm)
    tc = _pick(d, tc)
    tile = pl.BlockSpec((tm, tc), lambda i, j: (i, j))
    two = pl.BlockSpec((2, tm, tc), lambda i, j: (0, i, j))
    vec = pl.BlockSpec((1, tc), lambda i, j: (0, j))
    return pl.pallas_call(
        _rwkv_post_kernel,
        grid=(n // tm, d // tc),
        in_specs=[two, tile, tile, tile, two, tile, vec, vec, vec, vec],
        out_specs=tile,
        out_shape=jax.ShapeDtypeStruct((n, d), jnp.bfloat16),
        compiler_params=_params("parallel", "parallel"),
        name="rwkv_post",
    )(y, r, k, v, asig, g, ln_w.reshape(1, d), ln_b.reshape(1, d), k_a.reshape(1, d),
      r_k.reshape(1, d))


def rwkv_layer(x2, g_mix, mu, w_rkv, w0, w1, w2, a0, a1, a2, g1, g2, k_k, k_a, r_k, ln_w, ln_b,
               w_o, *, batch, seq):
    n, d = x2.shape
    bf = jnp.bfloat16
    xr, xk, xv, xw, xa, xg = rwkv_mix(x2, g_mix, mu, seq=seq)
    r = matmul(xr, w_rkv[0].astype(bf), name="rwkv_r")
    k = matmul(xk, w_rkv[1].astype(bf), name="rwkv_k")
    v = matmul(xv, w_rkv[2].astype(bf), name="rwkv_v")
    wl = matmul(xw, jnp.concatenate([w1[0], w1[1]], axis=1).astype(bf), epilogue="tanh",
                out_dtype=bf, name="rwkv_w1")
    al = matmul(xa, jnp.concatenate([a1[0], a1[1]], axis=1).astype(bf), out_dtype=bf,
                name="rwkv_a1")
    gpad = (-g1.shape[1]) % LANES
    gl = matmul(xg, jnp.pad(g1, ((0, 0), (0, gpad))).astype(bf), epilogue="sigmoid",
                out_dtype=bf, name="rwkv_g1")
    lw, asig, kk, g = rwkv_prep(k, wl, al, gl, w2.astype(bf), a2.astype(bf),
                                jnp.pad(g2, ((0, gpad), (0, 0))).astype(bf), w0, a0, k_k)
    shp = (batch, seq, d)
    y = rwkv_scan(r.reshape(shp), k.reshape(shp), v.reshape(shp), kk.reshape(shp),
                  lw.reshape((2,) + shp), asig.reshape((2,) + shp), k_a, batch=batch, seq=seq)
    o = rwkv_post(y.reshape(2, n, d), r, k, v, asig, g, ln_w, ln_b, k_a, r_k)
    return matmul(o, w_o.astype(bf), epilogue="residual", extra=x2, name="rwkv_out")


def kernel(x, norm_mix, norm_ffn, attn_w_qkv, attn_q_norm, attn_k_norm, attn_lambda, attn_sub_norm,
           attn_w_o, rwkv_mu, rwkv_w_rkv, rwkv_w0, rwkv_w1, rwkv_w2, rwkv_a0, rwkv_a1, rwkv_a2,
           rwkv_g1, rwkv_g2, rwkv_k_k, rwkv_k_a, rwkv_r_k, rwkv_ln_w, rwkv_ln_b, rwkv_w_o,
           peer_w_q, peer_sub_keys, peer_u, peer_v):
    batch, seq, d = x.shape
    depth = norm_mix.shape[0]
    x2 = x.reshape(batch * seq, d)
    for i in range(depth):
        j = i // N_MIXERS
        if i % N_MIXERS == 0:
            lambda_init = 0.8 - 0.6 * math.exp(-0.3 * i)
            x2 = differential_attention(x2, norm_mix[i], attn_w_qkv[j], attn_q_norm[j],
                                        attn_k_norm[j], attn_lambda[j], attn_sub_norm[j],
                                        attn_w_o[j], lambda_init, batch=batch, seq=seq)
        else:
            x2 = rwkv_layer(x2, norm_mix[i], rwkv_mu[j], rwkv_w_rkv[j], rwkv_w0[j], rwkv_w1[j],
                            rwkv_w2[j], rwkv_a0[j], rwkv_a1[j], rwkv_a2[j], rwkv_g1[j], rwkv_g2[j],
                            rwkv_k_k[j], rwkv_k_a[j], rwkv_r_k[j].reshape(-1), rwkv_ln_w[j],
                            rwkv_ln_b[j], rwkv_w_o[j], batch=batch, seq=seq)
        x2 = peer_layer(x2, norm_ffn[i], peer_w_q[i], peer_sub_keys[i], peer_u[i], peer_v[i])
    return x2.reshape(batch, seq, d)
```

```python
import functools
import math

import jax
import jax.numpy as jnp
from jax import lax
from jax.experimental import pallas as pl
from jax.experimental.pallas import tpu as pltpu

NORM_EPS = 1e-6
N_MIXERS = 2

DA_HEAD_DIM = 128
DA_V_DIM = 2 * DA_HEAD_DIM

RW_HEAD_SIZE = 64
RW_LNX_EPS = RW_HEAD_SIZE * 1e-5
RW_CHUNK = 64

PEER_HEADS = 8
PEER_NKEYS = 128
PEER_HALF = 128
PEER_TOPK = 16

V7X_VMEM_LIMIT_BYTES = 56 * 1024 * 1024
LANES = 128

_NT = (((1,), (1,)), ((), ()))
_TN = (((0,), (0,)), ((), ()))

_NEG_INF = float("-inf")


def _params(*sem):
    return pltpu.CompilerParams(dimension_semantics=sem,
                                vmem_limit_bytes=V7X_VMEM_LIMIT_BYTES)


def _pick(n, pref):
    if n <= pref:
        return n
    b = pref
    while n % b:
        b //= 2
    return b


def _rmsnorm_kernel(x_ref, g_ref, o_ref):
    x = x_ref[...]
    ms = jnp.mean(x * x, axis=-1, keepdims=True)
    o_ref[...] = (x * lax.rsqrt(ms + NORM_EPS) * g_ref[...]).astype(o_ref.dtype)


def rmsnorm_bf16(x, g):
    n, d = x.shape
    tm = _pick(n, 256)
    return pl.pallas_call(
        _rmsnorm_kernel,
        grid=(n // tm,),
        in_specs=[pl.BlockSpec((tm, d), lambda i: (i, 0)),
                  pl.BlockSpec((1, d), lambda i: (0, 0))],
        out_specs=pl.BlockSpec((tm, d), lambda i: (i, 0)),
        out_shape=jax.ShapeDtypeStruct((n, d), jnp.bfloat16),
        compiler_params=_params("parallel"),
        name="rmsnorm",
    )(x, g.reshape(1, d))


def _mm_kernel(*refs, epilogue, n_norm_tiles, nt):
    a_ref, b_ref = refs[0], refs[1]
    o_ref = refs[-1]
    if nt:
        acc = lax.dot_general(a_ref[...], b_ref[...], _NT, preferred_element_type=jnp.float32)
    else:
        acc = jnp.dot(a_ref[...], b_ref[...], preferred_element_type=jnp.float32)
    if epilogue == "plain":
        o_ref[...] = acc.astype(o_ref.dtype)
    elif epilogue == "residual":
        o_ref[...] = refs[2][...] + acc
    elif epilogue == "tanh":
        o_ref[...] = jnp.tanh(acc).astype(o_ref.dtype)
    elif epilogue == "sigmoid":
        o_ref[...] = jax.nn.sigmoid(acc).astype(o_ref.dtype)
    elif epilogue == "headnorm":
        gain_ref = refs[2]
        j = pl.program_id(1)

        @pl.when(j < n_norm_tiles)
        def _():
            for c in range(acc.shape[1] // DA_HEAD_DIM):
                sl = slice(c * DA_HEAD_DIM, (c + 1) * DA_HEAD_DIM)
                blk = acc[:, sl]
                ms = jnp.mean(blk * blk, axis=-1, keepdims=True)
                o_ref[:, sl] = (blk * lax.rsqrt(ms + NORM_EPS) * gain_ref[:, sl]).astype(o_ref.dtype)

        @pl.when(j >= n_norm_tiles)
        def _():
            o_ref[...] = acc.astype(o_ref.dtype)
    else:
        raise ValueError(epilogue)


def matmul(a, b, *, epilogue="plain", out_dtype=jnp.float32, extra=None,
           n_norm_cols=0, nt=False, tm=1024, tn=512, name="matmul"):
    m, k = a.shape
    n, k2 = (b.shape if nt else b.shape[::-1])
    assert k == k2
    tm = _pick(m, tm)
    tn = _pick(n, tn)
    in_specs = [pl.BlockSpec((tm, k), lambda i, j: (i, 0)),
                pl.BlockSpec((tn, k), lambda i, j: (j, 0)) if nt
                else pl.BlockSpec((k, tn), lambda i, j: (0, j))]
    args = [a, b]
    if epilogue == "residual":
        in_specs.append(pl.BlockSpec((tm, tn), lambda i, j: (i, j)))
        args.append(extra)
    elif epilogue == "headnorm":
        in_specs.append(pl.BlockSpec((1, tn), lambda i, j: (0, j)))
        args.append(extra.reshape(1, n))
        assert n_norm_cols % tn == 0
    kern = functools.partial(_mm_kernel, epilogue=epilogue,
                             n_norm_tiles=n_norm_cols // tn, nt=nt)
    return pl.pallas_call(
        kern,
        grid=(m // tm, n // tn),
        in_specs=in_specs,
        out_specs=pl.BlockSpec((tm, tn), lambda i, j: (i, j)),
        out_shape=jax.ShapeDtypeStruct((m, n), out_dtype),
        compiler_params=_params("parallel", "parallel"),
        name=name,
    )(*args)


ATTN_ROWS = 16


def _attn_kernel(par_ref, zero_ref, q_ref, k_ref, vt_ref, sub_ref, o_ref,
                 m_scr, l_scr, acc_scr, bias_scr, s0_scr, s1_scr, p0_scr, p1_scr,
                 *, tq, tk, tkb, out_scale):
    h = pl.program_id(1)
    qi = pl.program_id(2)
    kj = pl.program_id(3)
    d = DA_HEAD_DIM
    R = ATTN_ROWS
    n_chunk = tk // R
    n_kb = tkb // tk
    slope = par_ref[1 + h]

    @pl.when(kj == 0)
    def _():
        m_scr[...] = jnp.full(m_scr.shape, _NEG_INF, jnp.float32)
        l_scr[...] = jnp.zeros(l_scr.shape, jnp.float32)
        acc_scr[...] = jnp.zeros(acc_scr.shape, jnp.float32)
        for r in range(n_chunk):
            rows = slice(r * R, (r + 1) * R)
            rel = (lax.broadcasted_iota(jnp.int32, (R, tq), 0) + (r * R)
                   - lax.broadcasted_iota(jnp.int32, (R, tq), 1)).astype(jnp.float32)
            bias_scr[0, rows, :] = rel * (-slope)
            bias_scr[1, rows, :] = rel * slope
            bias_scr[2, rows, :] = jnp.abs(rel) * (-slope)

    tiles, shifts = [], []
    for kb in range(n_kb):
        off = kj * tkb + kb * tk - qi * tq
        tiles.append(jnp.where(off > 0, 0, jnp.where(off < 0, 1, 2)))
        shifts.append(jnp.abs(off).astype(jnp.float32) * (-slope))

    z0 = zero_ref[0]
    s_scr = (s0_scr, s1_scr)
    p_scr = (p0_scr, p1_scr)

    def rows_of(ref, start):
        return ref[pl.ds(pl.multiple_of(z0 + start, R), R), :]

    q = q_ref[...]

    def scores(c, kb):
        s_scr[c][kb * tk:(kb + 1) * tk, :] = lax.dot_general(
            k_ref[kb * tk:(kb + 1) * tk, c * d:(c + 1) * d], q[:, c * d:(c + 1) * d], _NT,
            preferred_element_type=jnp.float32)

    for kb in range(n_kb):
        scores(0, kb)
    for c in range(2):
        blk_max = jnp.full((8, tq), _NEG_INF, jnp.float32)
        for kb in range(n_kb):
            if c == 0:
                scores(1, kb)
            part = jnp.full((8, tq), _NEG_INF, jnp.float32)
            for r in range(n_chunk):
                s = rows_of(s_scr[c], kb * tk + r * R) + bias_scr[tiles[kb], r * R:(r + 1) * R, :]
                part = jnp.maximum(part, jnp.max(s.reshape(R // 8, 8, tq), axis=0))
            blk_max = jnp.maximum(blk_max, part + shifts[kb])
        m_prev = m_scr[c]
        m_new = jnp.maximum(m_prev, jnp.max(blk_max, axis=0, keepdims=True))
        alpha = jnp.exp2(m_prev - m_new)
        lsum = jnp.zeros((8, tq), jnp.float32)
        for kb in range(n_kb):
            m_kb = m_new - shifts[kb]
            for r in range(n_chunk):
                p = jnp.exp2(rows_of(s_scr[c], kb * tk + r * R)
                             + bias_scr[tiles[kb], r * R:(r + 1) * R, :] - m_kb)
                p_scr[c][kb * tk + r * R:kb * tk + (r + 1) * R, :] = p.astype(jnp.bfloat16)
                lsum = lsum + jnp.sum(p.reshape(R // 8, 8, tq), axis=0)
        l_scr[c] = alpha * l_scr[c] + jnp.sum(lsum, axis=0, keepdims=True)
        acc_scr[c] = acc_scr[c] * alpha + jnp.dot(vt_ref[...], p_scr[c][...],
                                                  preferred_element_type=jnp.float32)
        m_scr[c] = m_new

    @pl.when(kj == pl.num_programs(3) - 1)
    def _():
        lam = par_ref[0]
        ot = acc_scr[0] * (1.0 / l_scr[0]) - acc_scr[1] * (lam / l_scr[1])
        ms = jnp.mean(ot * ot, axis=0, keepdims=True)
        o = (ot * lax.rsqrt(ms + NORM_EPS)).T
        o_ref[...] = (o * (sub_ref[...] * out_scale)).astype(o_ref.dtype)


def diff_attention_core(qk, vt, par, sub_norm, *, batch, seq, n_heads, out_scale,
                        tq=512, tk=512, tkb=2048):
    n = batch * seq
    tq = _pick(seq, tq)
    tkb = _pick(seq, tkb)
    tk = tq
    assert tkb % tk == 0 and tk % ATTN_ROWS == 0
    nq, nk = seq // tq, seq // tkb
    vd = DA_V_DIM
    kern = functools.partial(_attn_kernel, tq=tq, tk=tk, tkb=tkb, out_scale=out_scale)
    return pl.pallas_call(
        kern,
        grid=(batch, n_heads, nq, nk),
        in_specs=[
            pl.BlockSpec(memory_space=pltpu.SMEM),
            pl.BlockSpec(memory_space=pltpu.SMEM),
            pl.BlockSpec((tq, vd), lambda b, h, i, j: (b * nq + i, h)),
            pl.BlockSpec((tkb, vd), lambda b, h, i, j: (b * nk + j, n_heads + h)),
            pl.BlockSpec((vd, tkb), lambda b, h, i, j: (h, b * nk + j)),
            pl.BlockSpec((1, vd), lambda b, h, i, j: (0, 0)),
        ],
        out_specs=pl.BlockSpec((tq, vd), lambda b, h, i, j: (b * nq + i, h)),
        out_shape=jax.ShapeDtypeStruct((n, n_heads * vd), jnp.bfloat16),
        scratch_shapes=[pltpu.VMEM((2, 1, tq), jnp.float32),
                        pltpu.VMEM((2, 1, tq), jnp.float32),
                        pltpu.VMEM((2, vd, tq), jnp.float32),
                        pltpu.VMEM((3, tk, tq), jnp.float32),
                        pltpu.VMEM((tkb, tq), jnp.float32),
                        pltpu.VMEM((tkb, tq), jnp.float32),
                        pltpu.VMEM((tkb, tq), jnp.bfloat16),
                        pltpu.VMEM((tkb, tq), jnp.bfloat16)],
        compiler_params=_params("parallel", "parallel", "parallel", "arbitrary"),
        name="diff_attn",
    )(par, jnp.zeros((1,), jnp.int32), qk, qk, vt, sub_norm.reshape(1, vd))


def differential_attention(x2, g_mix, w_qkv, q_norm, k_norm, lam_p, sub_norm, w_o,
                           lambda_init, *, batch, seq):
    n, dm = x2.shape
    d = DA_HEAD_DIM
    n_heads = w_qkv.shape[1] // (3 * 2 * d)
    h = rmsnorm_bf16(x2, g_mix)
    qk_cols = 2 * n_heads * 2 * d
    log2e = math.log2(math.e)
    gains = jnp.concatenate([jnp.tile(q_norm * (d ** -0.5 * log2e), 2 * n_heads),
                             jnp.tile(k_norm, 2 * n_heads)])
    w_bf = w_qkv.astype(jnp.bfloat16)
    qk = matmul(h, w_bf[:, :qk_cols], epilogue="headnorm", extra=gains,
                n_norm_cols=qk_cols, out_dtype=jnp.bfloat16, name="attn_qk")
    vt = matmul(w_bf[:, qk_cols:].T, h, nt=True, out_dtype=jnp.bfloat16, name="attn_vt")
    lp = lam_p.astype(jnp.float32)
    lam = jnp.exp(jnp.sum(lp[0] * lp[1])) - jnp.exp(jnp.sum(lp[2] * lp[3])) + lambda_init
    slopes = 2.0 ** (-8.0 * jnp.arange(1, n_heads + 1, dtype=jnp.float32) / n_heads) * log2e
    par = jnp.concatenate([lam.reshape(1), slopes]).astype(jnp.float32)
    o = diff_attention_core(qk, vt, par, sub_norm, batch=batch, seq=seq, n_heads=n_heads,
                            out_scale=1.0 - lambda_init)
    return matmul(o, w_o.astype(jnp.bfloat16), epilogue="residual", extra=x2, name="attn_out")


def _top_values(s, k):
    rows = lax.broadcasted_iota(jnp.int32, s.shape, 0)
    n_rows = s.shape[0]
    vals = []
    for _ in range(k):
        m = jnp.max(s, axis=0, keepdims=True)
        vals.append(m)
        first = jnp.min(jnp.where(s == m, rows, n_rows), axis=0, keepdims=True)
        s = jnp.where(rows == first, _NEG_INF, s)
    return vals


def _peer_route_kernel(q_ref, keys_ref, s1_ref, e1_ref, s2_ref, e2_ref, tau_ref):
    kk = PEER_TOPK
    half = PEER_HALF
    taus = []
    for p in range(PEER_HEADS):
        qa = q_ref[:, (2 * p) * half:(2 * p + 1) * half]
        qb = q_ref[:, (2 * p + 1) * half:(2 * p + 2) * half]
        s1 = lax.dot_general(keys_ref[p, 0], qa, _NT, preferred_element_type=jnp.float32,
                             precision=lax.Precision.HIGHEST)
        s2 = lax.dot_general(keys_ref[p, 1], qb, _NT, preferred_element_type=jnp.float32,
                             precision=lax.Precision.HIGHEST)
        a = _top_values(s1, kk)
        b = _top_values(s2, kk)
        cands = [a[i] + b[j] for i in range(kk) for j in range(kk) if (i + 1) * (j + 1) <= kk]
        pad = (-len(cands)) % 8
        cands += [jnp.full_like(a[0], _NEG_INF)] * pad
        top = _top_values(jnp.concatenate(cands, axis=0), kk)
        mx = top[0]
        z = jnp.zeros_like(mx)
        for t in top:
            z = z + jnp.exp(t - mx)
        taus.append(top[kk - 1])
        s1_ref[p] = s1
        s2_ref[p] = s2
        e1_ref[p] = jnp.exp(s1 - a[0])
        e2_ref[p] = jnp.exp(s2 - b[0]) / z
    tau_ref[...] = jnp.concatenate(taus, axis=0)


def peer_route(q, sub_keys, *, tn=256):
    n = q.shape[0]
    tn = _pick(n, tn)
    ph, nk = PEER_HEADS, PEER_NKEYS
    big = jax.ShapeDtypeStruct((ph, nk, n), jnp.float32)
    bspec = pl.BlockSpec((ph, nk, tn), lambda i: (0, 0, i))
    return pl.pallas_call(
        _peer_route_kernel,
        grid=(n // tn,),
        in_specs=[pl.BlockSpec((tn, q.shape[1]), lambda i: (i, 0)),
                  pl.BlockSpec(sub_keys.shape, lambda i: (0, 0, 0, 0))],
        out_specs=[bspec, bspec, bspec, bspec, pl.BlockSpec((ph, tn), lambda i: (0, i))],
        out_shape=[big, big, big, big, jax.ShapeDtypeStruct((ph, n), jnp.float32)],
        compiler_params=_params("parallel"),
        name="peer_route",
    )(q, sub_keys)


def _peer_dense_kernel(h_ref, u_ref, v_ref, x_ref, s1_ref, e1_ref, s2_ref, e2_ref, tau_ref,
                       o_ref, *, a_per_tile, a_rows):
    e = pl.program_id(1)

    @pl.when(e == 0)
    def _():
        o_ref[...] = x_ref[...]

    act = lax.dot_general(u_ref[...], h_ref[...], _NT, preferred_element_type=jnp.float32)
    act = 0.5 * act * (1.0 + lax.erf(act * (2.0 ** -0.5)))
    nk = PEER_NKEYS
    a_base = lax.rem(e * a_per_tile, a_rows)
    parts = []
    for ai in range(a_per_tile):
        a = a_base + ai
        w = None
        for p in range(PEER_HEADS):
            s1row = s1_ref[p, pl.ds(a, 1), :]
            e1row = e1_ref[p, pl.ds(a, 1), :]
            taurow = tau_ref[pl.ds(p, 1), :]
            sel = jnp.where(s1row + s2_ref[p] >= taurow, e2_ref[p], 0.0) * e1row
            w = sel if w is None else w + sel
        parts.append(act[ai * nk:(ai + 1) * nk, :] * w)
    pt = jnp.concatenate(parts, axis=0)
    o_ref[...] += jnp.dot(pt.T.astype(jnp.bfloat16), v_ref[...], preferred_element_type=jnp.float32)


def peer_dense(hf, u, v, x2, s1, e1, s2, e2, tau, *, tn=512, te=512):
    n, d = hf.shape
    n_exp = u.shape[0]
    tn = _pick(n, tn)
    te = _pick(n_exp, te)
    nk, ph = PEER_NKEYS, PEER_HEADS
    a_per_tile = te // nk
    a_rows = max(a_per_tile, 8)
    assert a_rows % a_per_tile == 0
    kern = functools.partial(_peer_dense_kernel, a_per_tile=a_per_tile, a_rows=a_rows)
    tok = lambda i, e: (i, 0)
    once = pl.Buffered(1)
    first = pl.BlockSpec((ph, a_rows, tn), lambda i, e: (0, (e * a_per_tile) // a_rows, i))
    second = pl.BlockSpec((ph, nk, tn), lambda i, e: (0, 0, i), pipeline_mode=once)
    return pl.pallas_call(
        kern,
        grid=(n // tn, n_exp // te),
        in_specs=[pl.BlockSpec((tn, d), tok, pipeline_mode=once),
                  pl.BlockSpec((te, d), lambda i, e: (e, 0)),
                  pl.BlockSpec((te, d), lambda i, e: (e, 0)),
                  pl.BlockSpec((tn, d), tok, pipeline_mode=once),
                  first, first, second, second,
                  pl.BlockSpec((ph, tn), lambda i, e: (0, i))],
        out_specs=pl.BlockSpec((tn, d), tok, pipeline_mode=once),
        out_shape=jax.ShapeDtypeStruct((n, d), jnp.float32),
        compiler_params=_params("parallel", "arbitrary"),
        name="peer_dense",
    )(hf, u, v, x2, s1, e1, s2, e2, tau)


def peer_layer(x2, g_ffn, w_q, sub_keys, u_tab, v_tab):
    hf = rmsnorm_bf16(x2, g_ffn)
    q = matmul(hf, w_q.astype(jnp.bfloat16), name="peer_q")
    s1, e1, s2, e2, tau = peer_route(q, sub_keys)
    return peer_dense(hf, u_tab.astype(jnp.bfloat16), v_tab.astype(jnp.bfloat16), x2,
                      s1, e1, s2, e2, tau)


def _rwkv_mix_kernel(x_ref, xp_ref, xn_ref, g_ref, mu_ref, *o_refs, tm, seq):
    i = pl.program_id(0)
    g = g_ref[...]

    def norm(x):
        ms = jnp.mean(x * x, axis=-1, keepdims=True)
        return x * lax.rsqrt(ms + NORM_EPS) * g

    h = norm(x_ref[...])
    hp = norm(xp_ref[...])[7:8, :]
    hn = norm(xn_ref[...])[0:1, :]
    t0 = lax.rem(i * tm, seq)
    prev_row = jnp.where(t0 == 0, 0.0, hp)
    next_row = jnp.where(t0 + tm == seq, 0.0, hn)
    rows = lax.broadcasted_iota(jnp.int32, h.shape, 0)
    h_prev = jnp.where(rows == 0, prev_row, pltpu.roll(h, 1, axis=0))
    h_next = jnp.where(rows == tm - 1, next_row, pltpu.roll(h, tm - 1, axis=0))
    xx = 0.5 * (h_prev + h_next) - h
    for idx, o_ref in enumerate(o_refs):
        o_ref[...] = (h + xx * mu_ref[idx:idx + 1, :]).astype(o_ref.dtype)


def rwkv_mix(x2, g_mix, mu, *, seq, tm=256):
    n, d = x2.shape
    tm = _pick(seq, tm)
    assert tm % 8 == 0 and seq % tm == 0
    r8 = tm // 8
    last = n // 8 - 1
    kern = functools.partial(_rwkv_mix_kernel, tm=tm, seq=seq)
    out = jax.ShapeDtypeStruct((n, d), jnp.bfloat16)
    return pl.pallas_call(
        kern,
        grid=(n // tm,),
        in_specs=[pl.BlockSpec((tm, d), lambda i: (i, 0)),
                  pl.BlockSpec((8, d), lambda i: (jnp.maximum(i * r8 - 1, 0), 0)),
                  pl.BlockSpec((8, d), lambda i: (jnp.minimum((i + 1) * r8, last), 0)),
                  pl.BlockSpec((1, d), lambda i: (0, 0)),
                  pl.BlockSpec((8, d), lambda i: (0, 0))],
        out_specs=[pl.BlockSpec((tm, d), lambda i: (i, 0))] * 6,
        out_shape=[out] * 6,
        compiler_params=_params("parallel"),
        name="rwkv_mix",
    )(x2, x2, x2, g_mix.reshape(1, d), jnp.pad(mu, ((0, 2), (0, 0))))


def _head_ones(width, scale):
    r = lax.broadcasted_iota(jnp.int32, (width, width), 0) // RW_HEAD_SIZE
    c = lax.broadcasted_iota(jnp.int32, (width, width), 1) // RW_HEAD_SIZE
    return jnp.where(r == c, scale, 0.0).astype(jnp.float32)


def _head_sum(x, ones):
    return jnp.dot(x, ones, preferred_element_type=jnp.float32,
                   precision=lax.Precision.HIGHEST)


def _rwkv_prep_kernel(k_ref, wl_ref, al_ref, gl_ref, w2_ref, a2_ref, g2_ref, w0_ref, a0_ref,
                      kkw_ref, lw_ref, as_ref, kk_ref, g_ref, *, lora):
    for z in range(2):
        sl = slice(z * lora, (z + 1) * lora)
        wz = w0_ref[z:z + 1, :] + jnp.dot(wl_ref[:, sl], w2_ref[z], preferred_element_type=jnp.float32)
        y = -wz
        softplus = jnp.maximum(y, 0.0) + jnp.log1p(jnp.exp(-jnp.abs(y)))
        lw_ref[z] = -jnp.exp(-softplus - 0.5)
        az = a0_ref[z:z + 1, :] + jnp.dot(al_ref[:, sl], a2_ref[z], preferred_element_type=jnp.float32)
        as_ref[z] = jax.nn.sigmoid(az)
    g_ref[...] = jnp.dot(gl_ref[...], g2_ref[...], preferred_element_type=jnp.float32)
    kr = k_ref[...] * kkw_ref[...]
    ss = _head_sum(kr * kr, _head_ones(kr.shape[1], 1.0))
    kk_ref[...] = kr / jnp.maximum(jnp.sqrt(ss), 1e-12)


def rwkv_prep(k, wl, al, gl, w2, a2, g2, w0, a0, k_k, *, tm=512, tc=512):
    n, d = k.shape
    tm = _pick(n, tm)
    tc = _pick(d, tc)
    lora = w2.shape[1]
    glw = gl.shape[1]
    kern = functools.partial(_rwkv_prep_kernel, lora=lora)
    row = lambda i, j: (i, 0)
    tile = lambda i, j: (i, j)
    two = jax.ShapeDtypeStruct((2, n, d), jnp.float32)
    one = jax.ShapeDtypeStruct((n, d), jnp.float32)
    return pl.pallas_call(
        kern,
        grid=(n // tm, d // tc),
        in_specs=[pl.BlockSpec((tm, tc), tile),
                  pl.BlockSpec((tm, 2 * lora), row),
                  pl.BlockSpec((tm, 2 * lora), row),
                  pl.BlockSpec((tm, glw), row),
                  pl.BlockSpec((2, lora, tc), lambda i, j: (0, 0, j)),
                  pl.BlockSpec((2, lora, tc), lambda i, j: (0, 0, j)),
                  pl.BlockSpec((glw, tc), lambda i, j: (0, j)),
                  pl.BlockSpec((2, tc), lambda i, j: (0, j)),
                  pl.BlockSpec((2, tc), lambda i, j: (0, j)),
                  pl.BlockSpec((1, tc), lambda i, j: (0, j))],
        out_specs=[pl.BlockSpec((2, tm, tc), lambda i, j: (0, i, j)),
                   pl.BlockSpec((2, tm, tc), lambda i, j: (0, i, j)),
                   pl.BlockSpec((tm, tc), tile),
                   pl.BlockSpec((tm, tc), tile)],
        out_shape=[two, two, one, one],
        compiler_params=_params("parallel", "parallel"),
        name="rwkv_prep",
    )(k, wl, al, gl, w2, a2, g2, w0, a0, k_k.reshape(1, d))


def _split3(x):
    hi = x.astype(jnp.bfloat16)
    r1 = x - hi.astype(jnp.float32)
    mid = r1.astype(jnp.bfloat16)
    lo = (r1 - mid.astype(jnp.float32)).astype(jnp.bfloat16)
    return hi, mid, lo


def _bdot(a, b):
    return jnp.dot(a.astype(jnp.bfloat16), b.astype(jnp.bfloat16),
                   preferred_element_type=jnp.float32)


def _rwkv_scan_kernel(r_ref, k_ref, v_ref, kk_ref, lw_ref, as_ref, ka_ref, y_ref, s_scr,
                      *, n_chunks, n_heads):
    z = pl.program_id(0)
    c = pl.program_id(3)
    L = RW_CHUNK
    N = RW_HEAD_SIZE
    bf = jnp.bfloat16

    @pl.when(c == 0)
    def _():
        s_scr[...] = jnp.zeros(s_scr.shape, jnp.float32)

    sgn = 1 - 2 * z
    row = lax.broadcasted_iota(jnp.int32, (L, 2 * L), 0)
    col = lax.broadcasted_iota(jnp.int32, (L, 2 * L), 1)
    col = jnp.where(col >= L, col - L, col)
    delta = (col - row) * sgn
    strict2 = delta < 0
    incl2 = delta <= 0
    tri = jnp.where(incl2[:, :L], 1.0, 0.0).astype(bf)
    ka = ka_ref[...]

    def chunk(ci, carry):
        cc = jnp.where(z == 0, ci, n_chunks - 1 - ci)
        rows = pl.ds(pl.multiple_of(cc * L, L), L)
        lw = lw_ref[rows, :]
        a_s = as_ref[rows, :]
        r = r_ref[rows, :]
        k = k_ref[rows, :]
        v = v_ref[rows, :]
        kk = kk_ref[rows, :]
        hi, mid, lo = _split3(lw)
        cw = (jnp.dot(tri, hi, preferred_element_type=jnp.float32)
              + jnp.dot(tri, mid, preferred_element_type=jnp.float32)
              + jnp.dot(tri, lo, preferred_element_type=jnp.float32))
        tot = jnp.sum(lw, axis=0, keepdims=True)
        w_inv = jnp.exp(-cw)
        w_rem = jnp.exp(tot - cw)
        kd = k * (1.0 + (a_s - 1.0) * ka)
        ba = kk * a_s
        rt = r * jnp.exp(cw)
        at = -kk * jnp.exp(cw - lw)
        bt = ba * w_inv
        kt = kd * w_inv
        bp = ba * w_rem
        kp = kd * w_rem
        w_tot = jnp.exp(tot)
        heads = range(n_heads)
        sls = [slice(j * N, (j + 1) * N) for j in heads]
        g_mat = [jnp.concatenate([at[:, sl], rt[:, sl]], axis=0).astype(bf) for sl in sls]
        bk = [jnp.concatenate([bt[:, sl], kt[:, sl]], axis=0).astype(bf) for sl in sls]
        bkp = [jnp.concatenate([bp[:, sl], kp[:, sl]], axis=0).astype(bf) for sl in sls]
        vj = [v[:, sl].astype(bf) for sl in sls]
        s_old = [s_scr[j] for j in heads]
        m1 = [lax.dot_general(g_mat[j], bk[j], _NT, preferred_element_type=jnp.float32)
              for j in heads]
        m2 = [lax.dot_general(g_mat[j], s_old[j].astype(bf), _NT, preferred_element_type=jnp.float32)
              for j in heads]
        a_all = [jnp.where(strict2, m1[j][:L, :], 0.0) for j in heads]
        r_all = [jnp.where(incl2, m1[j][L:, :], 0.0).astype(bf) for j in heads]
        x = [m2[j][:L, :] + _bdot(a_all[j][:, L:], vj[j]) for j in heads]
        p = [a_all[j][:, :L].astype(bf) for j in heads]
        n_stage = L.bit_length() - 1
        for st in range(n_stage):
            x = [x[j] + _bdot(p[j], x[j]) for j in heads]
            if st < n_stage - 1:
                p = [_bdot(p[j], p[j]).astype(bf) for j in heads]
        uv = [jnp.concatenate([x[j].astype(bf), vj[j]], axis=0) for j in heads]
        for j in heads:
            y_ref[rows, sls[j]] = m2[j][L:, :] + jnp.dot(r_all[j], uv[j],
                                                        preferred_element_type=jnp.float32)
        for j in heads:
            s_scr[j] = s_old[j] * w_tot[:, sls[j]] + lax.dot_general(
                uv[j], bkp[j], _TN, preferred_element_type=jnp.float32)
        return carry

    lax.fori_loop(0, n_chunks, chunk, 0)


def rwkv_scan(r, k, v, kk, lw, asig, k_a, *, batch, seq, heads_per_step=16, rows_per_step=256):
    d = r.shape[-1]
    N, L = RW_HEAD_SIZE, RW_CHUNK
    hw = _pick(d, heads_per_step * N)
    tb = _pick(seq, rows_per_step)
    assert tb % L == 0
    nt = seq // tb
    kern = functools.partial(_rwkv_scan_kernel, n_chunks=tb // L, n_heads=hw // N)

    def tmap(z, c):
        return c + z * (nt - 1 - 2 * c)

    shared = pl.BlockSpec((None, tb, hw), lambda z, b, g, c: (b, tmap(z, c), g))
    per_dir = pl.BlockSpec((None, None, tb, hw), lambda z, b, g, c: (z, b, tmap(z, c), g))
    return pl.pallas_call(
        kern,
        grid=(2, batch, d // hw, nt),
        in_specs=[shared, shared, shared, shared, per_dir, per_dir,
                  pl.BlockSpec((1, hw), lambda z, b, g, c: (0, g))],
        out_specs=per_dir,
        out_shape=jax.ShapeDtypeStruct((2, batch, seq, d), jnp.float32),
        scratch_shapes=[pltpu.VMEM((hw // N, N, N), jnp.float32)],
        compiler_params=_params("parallel", "parallel", "parallel", "arbitrary"),
        name="rwkv_scan",
    )(r, k, v, kk, lw, asig, k_a.reshape(1, d))


def _rwkv_post_kernel(y_ref, r_ref, k_ref, v_ref, as_ref, g_ref, lnw_ref, lnb_ref, ka_ref, rk_ref,
                      o_ref):
    y = y_ref[0] + y_ref[1]
    width = y.shape[1]
    avg = _head_ones(width, 1.0 / RW_HEAD_SIZE)
    mean = _head_sum(y, avg)
    yc = y - mean
    var = _head_sum(yc * yc, avg)
    yn = yc * lax.rsqrt(var + RW_LNX_EPS) * lnw_ref[...] + lnb_ref[...]
    k_sum = k_ref[...] * (2.0 + (as_ref[0] + as_ref[1] - 2.0) * ka_ref[...])
    bonus = _head_sum(r_ref[...] * k_sum * rk_ref[...], _head_ones(width, 1.0)) * v_ref[...]
    o_ref[...] = ((yn + bonus) * g_ref[...]).astype(o_ref.dtype)


def rwkv_post(y, r, k, v, asig, g, ln_w, ln_b, k_a, r_k, *, tm=256, tc=512):
    n, d = r.shape
    tm = _pick(n, tm)
    tc = _pick(d, tc)
    tile = pl.BlockSpec((tm, tc), lambda i, j: (i, j))
    two = pl.BlockSpec((2, tm, tc), lambda i, j: (0, i, j))
    vec = pl.BlockSpec((1, tc), lambda i, j: (0, j))
    return pl.pallas_call(
        _rwkv_post_kernel,
        grid=(n // tm, d // tc),
        in_specs=[two, tile, tile, tile, two, tile, vec, vec, vec, vec],
        out_specs=tile,
        out_shape=jax.ShapeDtypeStruct((n, d), jnp.bfloat16),
        compiler_params=_params("parallel", "parallel"),
        name="rwkv_post",
    )(y, r, k, v, asig, g, ln_w.reshape(1, d), ln_b.reshape(1, d), k_a.reshape(1, d),
      r_k.reshape(1, d))


def rwkv_layer(x2, g_mix, mu, w_rkv, w0, w1, w2, a0, a1, a2, g1, g2, k_k, k_a, r_k, ln_w, ln_b,
               w_o, *, batch, seq):
    n, d = x2.shape
    bf = jnp.bfloat16
    xr, xk, xv, xw, xa, xg = rwkv_mix(x2, g_mix, mu, seq=seq)
    r = matmul(xr, w_rkv[0].astype(bf), name="rwkv_r")
    k = matmul(xk, w_rkv[1].astype(bf), name="rwkv_k")
    v = matmul(xv, w_rkv[2].astype(bf), name="rwkv_v")
    wl = matmul(xw, jnp.concatenate([w1[0], w1[1]], axis=1).astype(bf), epilogue="tanh",
                out_dtype=bf, name="rwkv_w1")
    al = matmul(xa, jnp.concatenate([a1[0], a1[1]], axis=1).astype(bf), out_dtype=bf,
                name="rwkv_a1")
    gpad = (-g1.shape[1]) % LANES
    gl = matmul(xg, jnp.pad(g1, ((0, 0), (0, gpad))).astype(bf), epilogue="sigmoid",
                out_dtype=bf, name="rwkv_g1")
    lw, asig, kk, g = rwkv_prep(k, wl, al, gl, w2.astype(bf), a2.astype(bf),
                                jnp.pad(g2, ((0, gpad), (0, 0))).astype(bf), w0, a0, k_k)
    shp = (batch, seq, d)
    y = rwkv_scan(r.reshape(shp), k.reshape(shp), v.reshape(shp), kk.reshape(shp),
                  lw.reshape((2,) + shp), asig.reshape((2,) + shp), k_a, batch=batch, seq=seq)
    o = rwkv_post(y.reshape(2, n, d), r, k, v, asig, g, ln_w, ln_b, k_a, r_k)
    return matmul(o, w_o.astype(bf), epilogue="residual", extra=x2, name="rwkv_out")


def kernel(x, norm_mix, norm_ffn, attn_w_qkv, attn_q_norm, attn_k_norm, attn_lambda, attn_sub_norm,
           attn_w_o, rwkv_mu, rwkv_w_rkv, rwkv_w0, rwkv_w1, rwkv_w2, rwkv_a0, rwkv_a1, rwkv_a2,
           rwkv_g1, rwkv_g2, rwkv_k_k, rwkv_k_a, rwkv_r_k, rwkv_ln_w, rwkv_ln_b, rwkv_w_o,
           peer_w_q, peer_sub_keys, peer_u, peer_v):
    batch, seq, d = x.shape
    depth = norm_mix.shape[0]
    x2 = x.reshape(batch * seq, d)
    for i in range(depth):
        j = i // N_MIXERS
        if i % N_MIXERS == 0:
            lambda_init = 0.8 - 0.6 * math.exp(-0.3 * i)
            x2 = differential_attention(x2, norm_mix[i], attn_w_qkv[j], attn_q_norm[j],
                                        attn_k_norm[j], attn_lambda[j], attn_sub_norm[j],
                                        attn_w_o[j], lambda_init, batch=batch, seq=seq)
        else:
            x2 = rwkv_layer(x2, norm_mix[i], rwkv_mu[j], rwkv_w_rkv[j], rwkv_w0[j], rwkv_w1[j],
                            rwkv_w2[j], rwkv_a0[j], rwkv_a1[j], rwkv_a2[j], rwkv_g1[j], rwkv_g2[j],
                            rwkv_k_k[j], rwkv_k_a[j], rwkv_r_k[j].reshape(-1), rwkv_ln_w[j],
                            rwkv_ln_b[j], rwkv_w_o[j], batch=batch, seq=seq)
        x2 = peer_layer(x2, norm_ffn[i], peer_w_q[i], peer_sub_keys[i], peer_u[i], peer_v[i])
    return x2.reshape(batch, seq, d)
```

```python
import functools
import math

import jax
import jax.numpy as jnp
from jax import lax
from jax.experimental import pallas as pl
from jax.experimental.pallas import tpu as pltpu

NORM_EPS = 1e-6
N_MIXERS = 2

DA_HEAD_DIM = 128
DA_V_DIM = 2 * DA_HEAD_DIM

RW_HEAD_SIZE = 64
RW_LNX_EPS = RW_HEAD_SIZE * 1e-5
RW_CHUNK = 64

PEER_HEADS = 8
PEER_NKEYS = 128
PEER_HALF = 128
PEER_TOPK = 16

V7X_VMEM_LIMIT_BYTES = 56 * 1024 * 1024
LANES = 128
SUBLANES = 8

_NT = (((1,), (1,)), ((), ()))
_TN = (((0,), (0,)), ((), ()))

_NEG_INF = float("-inf")


def _params(*sem):
    return pltpu.CompilerParams(dimension_semantics=sem,
                                vmem_limit_bytes=V7X_VMEM_LIMIT_BYTES)


def _pick(n, pref):
    if n <= pref:
        return n
    b = pref
    while n % b:
        b //= 2
    return b


def _rmsnorm_kernel(x_ref, g_ref, o_ref):
    x = x_ref[...]
    ms = jnp.mean(x * x, axis=-1, keepdims=True)
    o_ref[...] = (x * lax.rsqrt(ms + NORM_EPS) * g_ref[...]).astype(o_ref.dtype)


def rmsnorm_bf16(x, g):
    n, d = x.shape
    tm = _pick(n, 256)
    return pl.pallas_call(
        _rmsnorm_kernel,
        grid=(n // tm,),
        in_specs=[pl.BlockSpec((tm, d), lambda i: (i, 0)),
                  pl.BlockSpec((1, d), lambda i: (0, 0))],
        out_specs=pl.BlockSpec((tm, d), lambda i: (i, 0)),
        out_shape=jax.ShapeDtypeStruct((n, d), jnp.bfloat16),
        compiler_params=_params("parallel"),
        name="rmsnorm",
    )(x, g.reshape(1, d))


def _mm_kernel(*refs, epilogue, n_norm_tiles, nt):
    a_ref, b_ref = refs[0], refs[1]
    o_ref = refs[-1]
    if nt:
        acc = lax.dot_general(a_ref[...], b_ref[...], _NT, preferred_element_type=jnp.float32)
    else:
        acc = jnp.dot(a_ref[...], b_ref[...], preferred_element_type=jnp.float32)
    if epilogue == "plain":
        o_ref[...] = acc.astype(o_ref.dtype)
    elif epilogue == "residual":
        o_ref[...] = refs[2][...] + acc
    elif epilogue == "tanh":
        o_ref[...] = jnp.tanh(acc).astype(o_ref.dtype)
    elif epilogue == "sigmoid":
        o_ref[...] = jax.nn.sigmoid(acc).astype(o_ref.dtype)
    elif epilogue == "headnorm":
        gain_ref = refs[2]
        j = pl.program_id(1)

        @pl.when(j < n_norm_tiles)
        def _():
            for c in range(acc.shape[1] // DA_HEAD_DIM):
                sl = slice(c * DA_HEAD_DIM, (c + 1) * DA_HEAD_DIM)
                blk = acc[:, sl]
                ms = jnp.mean(blk * blk, axis=-1, keepdims=True)
                o_ref[:, sl] = (blk * lax.rsqrt(ms + NORM_EPS) * gain_ref[:, sl]).astype(o_ref.dtype)

        @pl.when(j >= n_norm_tiles)
        def _():
            o_ref[...] = acc.astype(o_ref.dtype)
    else:
        raise ValueError(epilogue)


def matmul(a, b, *, epilogue="plain", out_dtype=jnp.float32, extra=None,
           n_norm_cols=0, nt=False, tm=1024, tn=512, name="matmul"):
    m, k = a.shape
    n, k2 = (b.shape if nt else b.shape[::-1])
    assert k == k2
    tm = _pick(m, tm)
    tn = _pick(n, tn)
    in_specs = [pl.BlockSpec((tm, k), lambda i, j: (i, 0)),
                pl.BlockSpec((tn, k), lambda i, j: (j, 0)) if nt
                else pl.BlockSpec((k, tn), lambda i, j: (0, j))]
    args = [a, b]
    if epilogue == "residual":
        in_specs.append(pl.BlockSpec((tm, tn), lambda i, j: (i, j)))
        args.append(extra)
    elif epilogue == "headnorm":
        in_specs.append(pl.BlockSpec((1, tn), lambda i, j: (0, j)))
        args.append(extra.reshape(1, n))
        assert n_norm_cols % tn == 0
    kern = functools.partial(_mm_kernel, epilogue=epilogue,
                             n_norm_tiles=n_norm_cols // tn, nt=nt)
    return pl.pallas_call(
        kern,
        grid=(m // tm, n // tn),
        in_specs=in_specs,
        out_specs=pl.BlockSpec((tm, tn), lambda i, j: (i, j)),
        out_shape=jax.ShapeDtypeStruct((m, n), out_dtype),
        compiler_params=_params("parallel", "parallel"),
        name=name,
    )(*args)


ATTN_ROWS = 16


def _attn_kernel(par_ref, zero_ref, q_ref, k_ref, vt_ref, sub_ref, o_ref,
                 m_scr, l_scr, acc_scr, bias_scr, s0_scr, s1_scr, p0_scr, p1_scr,
                 *, tq, tk, tkb, out_scale):
    h = pl.program_id(1)
    qi = pl.program_id(2)
    kj = pl.program_id(3)
    d = DA_HEAD_DIM
    R = ATTN_ROWS
    n_chunk = tk // R
    n_kb = tkb // tk
    slope = par_ref[1 + h]

    @pl.when(kj == 0)
    def _():
        m_scr[...] = jnp.full(m_scr.shape, _NEG_INF, jnp.float32)
        l_scr[...] = jnp.zeros(l_scr.shape, jnp.float32)
        acc_scr[...] = jnp.zeros(acc_scr.shape, jnp.float32)
        for r in range(n_chunk):
            rows = slice(r * R, (r + 1) * R)
            rel = (lax.broadcasted_iota(jnp.int32, (R, tq), 0) + (r * R)
                   - lax.broadcasted_iota(jnp.int32, (R, tq), 1)).astype(jnp.float32)
            bias_scr[0, rows, :] = rel * (-slope)
            bias_scr[1, rows, :] = rel * slope
            bias_scr[2, rows, :] = jnp.abs(rel) * (-slope)

    tiles, shifts = [], []
    for kb in range(n_kb):
        off = kj * tkb + kb * tk - qi * tq
        tiles.append(jnp.where(off > 0, 0, jnp.where(off < 0, 1, 2)))
        shifts.append(jnp.abs(off).astype(jnp.float32) * (-slope))

    z0 = zero_ref[0]
    s_scr = (s0_scr, s1_scr)
    p_scr = (p0_scr, p1_scr)

    def rows_of(ref, start):
        return ref[pl.ds(pl.multiple_of(z0 + start, R), R), :]

    q = q_ref[...]

    def scores(c, kb):
        sv = lax.dot_general(k_ref[kb * tk:(kb + 1) * tk, c * d:(c + 1) * d], q[:, c * d:(c + 1) * d],
                             _NT, preferred_element_type=jnp.float32)
        s_scr[c][kb * tk:(kb + 1) * tk, :] = sv
        part = jnp.full((SUBLANES, tq), _NEG_INF, jnp.float32)
        for r in range(n_chunk):
            s = sv[r * R:(r + 1) * R, :] + bias_scr[tiles[kb], r * R:(r + 1) * R, :]
            part = jnp.maximum(part, jnp.max(s.reshape(R // SUBLANES, SUBLANES, tq), axis=0))
        return part + shifts[kb]

    blk_maxes = []
    for c in range(2):
        blk_max = scores(c, 0)
        for kb in range(1, n_kb):
            blk_max = jnp.maximum(blk_max, scores(c, kb))
        blk_maxes.append(blk_max)
    for c in range(2):
        m_prev = m_scr[c]
        m_new = jnp.maximum(m_prev, jnp.max(blk_maxes[c], axis=0, keepdims=True))
        alpha = jnp.exp2(m_prev - m_new)
        lsum = jnp.zeros((SUBLANES, tq), jnp.float32)
        for kb in range(n_kb):
            m_kb = m_new - shifts[kb]
            for r in range(n_chunk):
                p = jnp.exp2(rows_of(s_scr[c], kb * tk + r * R)
                             + bias_scr[tiles[kb], r * R:(r + 1) * R, :] - m_kb)
                p_scr[c][kb * tk + r * R:kb * tk + (r + 1) * R, :] = p.astype(jnp.bfloat16)
                lsum = lsum + jnp.sum(p.reshape(R // SUBLANES, SUBLANES, tq), axis=0)
        l_scr[c] = alpha * l_scr[c] + jnp.sum(lsum, axis=0, keepdims=True)
        acc_scr[c] = acc_scr[c] * alpha + jnp.dot(vt_ref[...], p_scr[c][...],
                                                  preferred_element_type=jnp.float32)
        m_scr[c] = m_new

    @pl.when(kj == pl.num_programs(3) - 1)
    def _():
        lam = par_ref[0]
        ot = acc_scr[0] * (1.0 / l_scr[0]) - acc_scr[1] * (lam / l_scr[1])
        ms = jnp.mean(ot * ot, axis=0, keepdims=True)
        o = (ot * lax.rsqrt(ms + NORM_EPS)).T
        o_ref[...] = (o * (sub_ref[...] * out_scale)).astype(o_ref.dtype)


def diff_attention_core(qk, vt, par, sub_norm, *, batch, seq, n_heads, out_scale,
                        tq=512, tk=512, tkb=2048):
    n = batch * seq
    tq = _pick(seq, tq)
    tkb = _pick(seq, tkb)
    tk = tq
    assert tkb % tk == 0 and tk % ATTN_ROWS == 0
    nq, nk = seq // tq, seq // tkb
    vd = DA_V_DIM
    kern = functools.partial(_attn_kernel, tq=tq, tk=tk, tkb=tkb, out_scale=out_scale)
    return pl.pallas_call(
        kern,
        grid=(batch, n_heads, nq, nk),
        in_specs=[
            pl.BlockSpec(memory_space=pltpu.SMEM),
            pl.BlockSpec(memory_space=pltpu.SMEM),
            pl.BlockSpec((tq, vd), lambda b, h, i, j: (b * nq + i, h)),
            pl.BlockSpec((tkb, vd), lambda b, h, i, j: (b * nk + j, n_heads + h)),
            pl.BlockSpec((vd, tkb), lambda b, h, i, j: (h, b * nk + j)),
            pl.BlockSpec((1, vd), lambda b, h, i, j: (0, 0)),
        ],
        out_specs=pl.BlockSpec((tq, vd), lambda b, h, i, j: (b * nq + i, h)),
        out_shape=jax.ShapeDtypeStruct((n, n_heads * vd), jnp.bfloat16),
        scratch_shapes=[pltpu.VMEM((2, 1, tq), jnp.float32),
                        pltpu.VMEM((2, 1, tq), jnp.float32),
                        pltpu.VMEM((2, vd, tq), jnp.float32),
                        pltpu.VMEM((3, tk, tq), jnp.float32),
                        pltpu.VMEM((tkb, tq), jnp.float32),
                        pltpu.VMEM((tkb, tq), jnp.float32),
                        pltpu.VMEM((tkb, tq), jnp.bfloat16),
                        pltpu.VMEM((tkb, tq), jnp.bfloat16)],
        compiler_params=_params("parallel", "parallel", "parallel", "arbitrary"),
        name="diff_attn",
    )(par, jnp.zeros((1,), jnp.int32), qk, qk, vt, sub_norm.reshape(1, vd))


def differential_attention(x2, g_mix, w_qkv, q_norm, k_norm, lam_p, sub_norm, w_o,
                           lambda_init, *, batch, seq):
    n, dm = x2.shape
    d = DA_HEAD_DIM
    n_heads = w_qkv.shape[1] // (3 * 2 * d)
    h = rmsnorm_bf16(x2, g_mix)
    qk_cols = 2 * n_heads * 2 * d
    log2e = math.log2(math.e)
    gains = jnp.concatenate([jnp.tile(q_norm * (d ** -0.5 * log2e), 2 * n_heads),
                             jnp.tile(k_norm, 2 * n_heads)])
    w_bf = w_qkv.astype(jnp.bfloat16)
    qk = matmul(h, w_bf[:, :qk_cols], epilogue="headnorm", extra=gains,
                n_norm_cols=qk_cols, out_dtype=jnp.bfloat16, name="attn_qk")
    vt = matmul(w_bf[:, qk_cols:].T, h, nt=True, out_dtype=jnp.bfloat16, name="attn_vt")
    lp = lam_p.astype(jnp.float32)
    lam = jnp.exp(jnp.sum(lp[0] * lp[1])) - jnp.exp(jnp.sum(lp[2] * lp[3])) + lambda_init
    slopes = 2.0 ** (-8.0 * jnp.arange(1, n_heads + 1, dtype=jnp.float32) / n_heads) * log2e
    par = jnp.concatenate([lam.reshape(1), slopes]).astype(jnp.float32)
    o = diff_attention_core(qk, vt, par, sub_norm, batch=batch, seq=seq, n_heads=n_heads,
                            out_scale=1.0 - lambda_init)
    return matmul(o, w_o.astype(jnp.bfloat16), epilogue="residual", extra=x2, name="attn_out")


def _top_values(s, k):
    rows = lax.broadcasted_iota(jnp.int32, s.shape, 0)
    n_rows = s.shape[0]
    vals = []
    for _ in range(k):
        m = jnp.max(s, axis=0, keepdims=True)
        vals.append(m)
        first = jnp.min(jnp.where(s == m, rows, n_rows), axis=0, keepdims=True)
        s = jnp.where(rows == first, _NEG_INF, s)
    return vals


def _peer_route_kernel(q_ref, keys_ref, s1_ref, e1_ref, s2_ref, e2_ref, tau_ref):
    kk = PEER_TOPK
    half = PEER_HALF
    taus = []
    for p in range(PEER_HEADS):
        qa = q_ref[:, (2 * p) * half:(2 * p + 1) * half]
        qb = q_ref[:, (2 * p + 1) * half:(2 * p + 2) * half]
        s1 = lax.dot_general(keys_ref[p, 0], qa, _NT, preferred_element_type=jnp.float32,
                             precision=lax.Precision.HIGHEST)
        s2 = lax.dot_general(keys_ref[p, 1], qb, _NT, preferred_element_type=jnp.float32,
                             precision=lax.Precision.HIGHEST)
        a = _top_values(s1, kk)
        b = _top_values(s2, kk)
        cands = [a[i] + b[j] for i in range(kk) for j in range(kk) if (i + 1) * (j + 1) <= kk]
        pad = (-len(cands)) % SUBLANES
        cands += [jnp.full_like(a[0], _NEG_INF)] * pad
        top = _top_values(jnp.concatenate(cands, axis=0), kk)
        mx = top[0]
        z = jnp.zeros_like(mx)
        for t in top:
            z = z + jnp.exp(t - mx)
        taus.append(top[kk - 1])
        s1_ref[p] = s1
        s2_ref[p] = s2
        e1_ref[p] = jnp.exp(s1 - a[0])
        e2_ref[p] = jnp.exp(s2 - b[0]) / z
    tau_ref[...] = jnp.concatenate(taus, axis=0)


def peer_route(q, sub_keys, *, tn=256):
    n = q.shape[0]
    tn = _pick(n, tn)
    ph, nk = PEER_HEADS, PEER_NKEYS
    big = jax.ShapeDtypeStruct((ph, nk, n), jnp.float32)
    bspec = pl.BlockSpec((ph, nk, tn), lambda i: (0, 0, i))
    return pl.pallas_call(
        _peer_route_kernel,
        grid=(n // tn,),
        in_specs=[pl.BlockSpec((tn, q.shape[1]), lambda i: (i, 0)),
                  pl.BlockSpec(sub_keys.shape, lambda i: (0, 0, 0, 0))],
        out_specs=[bspec, bspec, bspec, bspec, pl.BlockSpec((ph, tn), lambda i: (0, i))],
        out_shape=[big, big, big, big, jax.ShapeDtypeStruct((ph, n), jnp.float32)],
        compiler_params=_params("parallel"),
        name="peer_route",
    )(q, sub_keys)


def _peer_dense_kernel(h_ref, u_ref, v_ref, x_ref, s1_ref, e1_ref, s2_ref, e2_ref, tau_ref,
                       o_ref, *, a_per_tile, a_rows):
    e = pl.program_id(1)

    @pl.when(e == 0)
    def _():
        o_ref[...] = x_ref[...]

    act = lax.dot_general(u_ref[...], h_ref[...], _NT, preferred_element_type=jnp.float32)
    act = 0.5 * act * (1.0 + lax.erf(act * (2.0 ** -0.5)))
    nk = PEER_NKEYS
    a_base = lax.rem(e * a_per_tile, a_rows)
    parts = []
    for ai in range(a_per_tile):
        a = a_base + ai
        w = None
        for p in range(PEER_HEADS):
            s1row = s1_ref[p, pl.ds(a, 1), :]
            e1row = e1_ref[p, pl.ds(a, 1), :]
            taurow = tau_ref[pl.ds(p, 1), :]
            sel = jnp.where(s1row + s2_ref[p] >= taurow, e2_ref[p], 0.0) * e1row
            w = sel if w is None else w + sel
        parts.append(act[ai * nk:(ai + 1) * nk, :] * w)
    pt = jnp.concatenate(parts, axis=0)
    o_ref[...] += jnp.dot(pt.T.astype(jnp.bfloat16), v_ref[...], preferred_element_type=jnp.float32)


def peer_dense(hf, u, v, x2, s1, e1, s2, e2, tau, *, tn=512, te=512):
    n, d = hf.shape
    n_exp = u.shape[0]
    tn = _pick(n, tn)
    te = _pick(n_exp, te)
    nk, ph = PEER_NKEYS, PEER_HEADS
    a_per_tile = te // nk
    a_rows = max(a_per_tile, SUBLANES)
    assert a_rows % a_per_tile == 0
    kern = functools.partial(_peer_dense_kernel, a_per_tile=a_per_tile, a_rows=a_rows)
    tok = lambda i, e: (i, 0)
    once = pl.Buffered(1)
    first = pl.BlockSpec((ph, a_rows, tn), lambda i, e: (0, (e * a_per_tile) // a_rows, i))
    second = pl.BlockSpec((ph, nk, tn), lambda i, e: (0, 0, i), pipeline_mode=once)
    return pl.pallas_call(
        kern,
        grid=(n // tn, n_exp // te),
        in_specs=[pl.BlockSpec((tn, d), tok, pipeline_mode=once),
                  pl.BlockSpec((te, d), lambda i, e: (e, 0)),
                  pl.BlockSpec((te, d), lambda i, e: (e, 0)),
                  pl.BlockSpec((tn, d), tok, pipeline_mode=once),
                  first, first, second, second,
                  pl.BlockSpec((ph, tn), lambda i, e: (0, i))],
        out_specs=pl.BlockSpec((tn, d), tok, pipeline_mode=once),
        out_shape=jax.ShapeDtypeStruct((n, d), jnp.float32),
        compiler_params=_params("parallel", "arbitrary"),
        name="peer_dense",
    )(hf, u, v, x2, s1, e1, s2, e2, tau)


def peer_layer(x2, g_ffn, w_q, sub_keys, u_tab, v_tab):
    hf = rmsnorm_bf16(x2, g_ffn)
    q = matmul(hf, w_q.astype(jnp.bfloat16), name="peer_q")
    s1, e1, s2, e2, tau = peer_route(q, sub_keys)
    return peer_dense(hf, u_tab.astype(jnp.bfloat16), v_tab.astype(jnp.bfloat16), x2,
                      s1, e1, s2, e2, tau)


def _rwkv_mix_kernel(x_ref, xp_ref, xn_ref, g_ref, mu_ref, *o_refs, tm, seq):
    i = pl.program_id(0)
    g = g_ref[...]

    def norm(x):
        ms = jnp.mean(x * x, axis=-1, keepdims=True)
        return x * lax.rsqrt(ms + NORM_EPS) * g

    h = norm(x_ref[...])
    hp = norm(xp_ref[...])[SUBLANES - 1:SUBLANES, :]
    hn = norm(xn_ref[...])[0:1, :]
    t0 = lax.rem(i * tm, seq)
    prev_row = jnp.where(t0 == 0, 0.0, hp)
    next_row = jnp.where(t0 + tm == seq, 0.0, hn)
    rows = lax.broadcasted_iota(jnp.int32, h.shape, 0)
    h_prev = jnp.where(rows == 0, prev_row, pltpu.roll(h, 1, axis=0))
    h_next = jnp.where(rows == tm - 1, next_row, pltpu.roll(h, tm - 1, axis=0))
    xx = 0.5 * (h_prev + h_next) - h
    for idx, o_ref in enumerate(o_refs):
        o_ref[...] = (h + xx * mu_ref[idx:idx + 1, :]).astype(o_ref.dtype)


def rwkv_mix(x2, g_mix, mu, *, seq, tm=256):
    n, d = x2.shape
    tm = _pick(seq, tm)
    assert tm % SUBLANES == 0 and seq % tm == 0
    r8 = tm // SUBLANES
    last = n // SUBLANES - 1
    kern = functools.partial(_rwkv_mix_kernel, tm=tm, seq=seq)
    out = jax.ShapeDtypeStruct((n, d), jnp.bfloat16)
    return pl.pallas_call(
        kern,
        grid=(n // tm,),
        in_specs=[pl.BlockSpec((tm, d), lambda i: (i, 0)),
                  pl.BlockSpec((SUBLANES, d), lambda i: (jnp.maximum(i * r8 - 1, 0), 0)),
                  pl.BlockSpec((SUBLANES, d), lambda i: (jnp.minimum((i + 1) * r8, last), 0)),
                  pl.BlockSpec((1, d), lambda i: (0, 0)),
                  pl.BlockSpec((SUBLANES, d), lambda i: (0, 0))],
        out_specs=[pl.BlockSpec((tm, d), lambda i: (i, 0))] * 6,
        out_shape=[out] * 6,
        compiler_params=_params("parallel"),
        name="rwkv_mix",
    )(x2, x2, x2, g_mix.reshape(1, d), jnp.pad(mu, ((0, SUBLANES - mu.shape[0]), (0, 0))))


HEAD_SUM_WIDTH = 256


def _split3(x):
    hi = x.astype(jnp.bfloat16)
    r1 = x - hi.astype(jnp.float32)
    mid = r1.astype(jnp.bfloat16)
    lo = (r1 - mid.astype(jnp.float32)).astype(jnp.bfloat16)
    return hi, mid, lo


def _head_sum(x, scale):
    w = min(HEAD_SUM_WIDTH, x.shape[1])
    r = lax.broadcasted_iota(jnp.int32, (w, w), 0) // RW_HEAD_SIZE
    c = lax.broadcasted_iota(jnp.int32, (w, w), 1) // RW_HEAD_SIZE
    ones = jnp.where(r == c, scale, 0.0).astype(jnp.bfloat16)
    out = []
    for g in range(x.shape[1] // w):
        terms = _split3(x[:, g * w:(g + 1) * w])
        out.append(sum(jnp.dot(t, ones, preferred_element_type=jnp.float32) for t in terms))
    return out[0] if len(out) == 1 else jnp.concatenate(out, axis=1)


def _rwkv_prep_kernel(k_ref, wl_ref, al_ref, gl_ref, w2_ref, a2_ref, g2_ref, w0_ref, a0_ref,
                      kkw_ref, lw_ref, as_ref, kk_ref, g_ref, *, lora):
    for z in range(2):
        sl = slice(z * lora, (z + 1) * lora)
        wz = w0_ref[z:z + 1, :] + jnp.dot(wl_ref[:, sl], w2_ref[z], preferred_element_type=jnp.float32)
        y = -wz
        softplus = jnp.maximum(y, 0.0) + jnp.log1p(jnp.exp(-jnp.abs(y)))
        lw_ref[z] = -jnp.exp(-softplus - 0.5)
        az = a0_ref[z:z + 1, :] + jnp.dot(al_ref[:, sl], a2_ref[z], preferred_element_type=jnp.float32)
        as_ref[z] = jax.nn.sigmoid(az)
    g_ref[...] = jnp.dot(gl_ref[...], g2_ref[...], preferred_element_type=jnp.float32)
    kr = k_ref[...] * kkw_ref[...]
    ss = _head_sum(kr * kr, 1.0)
    kk_ref[...] = kr / jnp.maximum(jnp.sqrt(ss), 1e-12)


def rwkv_prep(k, wl, al, gl, w2, a2, g2, w0, a0, k_k, *, tm=512, tc=512):
    n, d = k.shape
    tm = _pick(n, tm)
    tc = _pick(d, tc)
    lora = w2.shape[1]
    glw = gl.shape[1]
    kern = functools.partial(_rwkv_prep_kernel, lora=lora)
    row = lambda i, j: (i, 0)
    tile = lambda i, j: (i, j)
    two = jax.ShapeDtypeStruct((2, n, d), jnp.float32)
    one = jax.ShapeDtypeStruct((n, d), jnp.float32)
    return pl.pallas_call(
        kern,
        grid=(n // tm, d // tc),
        in_specs=[pl.BlockSpec((tm, tc), tile),
                  pl.BlockSpec((tm, 2 * lora), row),
                  pl.BlockSpec((tm, 2 * lora), row),
                  pl.BlockSpec((tm, glw), row),
                  pl.BlockSpec((2, lora, tc), lambda i, j: (0, 0, j)),
                  pl.BlockSpec((2, lora, tc), lambda i, j: (0, 0, j)),
                  pl.BlockSpec((glw, tc), lambda i, j: (0, j)),
                  pl.BlockSpec((2, tc), lambda i, j: (0, j)),
                  pl.BlockSpec((2, tc), lambda i, j: (0, j)),
                  pl.BlockSpec((1, tc), lambda i, j: (0, j))],
        out_specs=[pl.BlockSpec((2, tm, tc), lambda i, j: (0, i, j)),
                   pl.BlockSpec((2, tm, tc), lambda i, j: (0, i, j)),
                   pl.BlockSpec((tm, tc), tile),
                   pl.BlockSpec((tm, tc), tile)],
        out_shape=[two, two, one, one],
        compiler_params=_params("parallel", "parallel"),
        name="rwkv_prep",
    )(k, wl, al, gl, w2, a2, g2, w0, a0, k_k.reshape(1, d))


def _bdot(a, b):
    return jnp.dot(a.astype(jnp.bfloat16), b.astype(jnp.bfloat16),
                   preferred_element_type=jnp.float32)


def _rwkv_scan_kernel(r_ref, k_ref, v_ref, kk_ref, lw_ref, as_ref, ka_ref, y_ref, s_scr,
                      *, n_chunks, n_heads, n_batch):
    z = pl.program_id(0)
    c = pl.program_id(2)
    L = RW_CHUNK
    N = RW_HEAD_SIZE
    bf = jnp.bfloat16

    @pl.when(c == 0)
    def _():
        s_scr[...] = jnp.zeros(s_scr.shape, jnp.float32)

    sgn = 1 - 2 * z
    row = lax.broadcasted_iota(jnp.int32, (L, 2 * L), 0)
    col = lax.broadcasted_iota(jnp.int32, (L, 2 * L), 1)
    col = jnp.where(col >= L, col - L, col)
    delta = (col - row) * sgn
    strict2 = delta < 0
    incl2 = delta <= 0
    tri = jnp.where(incl2[:, :L], 1.0, 0.0).astype(bf)
    ka = ka_ref[...]
    sls = [slice(j * N, (j + 1) * N) for j in range(n_heads)]
    chains = [(bi, j) for bi in range(n_batch) for j in range(n_heads)]
    ids = range(len(chains))

    def chunk(ci, carry):
        cc = jnp.where(z == 0, ci, n_chunks - 1 - ci)
        rows = pl.ds(pl.multiple_of(cc * L, L), L)
        g_mat, bk, bkp, vj, w_tot = [], [], [], [], []
        for bi in range(n_batch):
            lw = lw_ref[bi, rows, :]
            a_s = as_ref[bi, rows, :]
            kk = kk_ref[bi, rows, :]
            hi, mid, lo = _split3(lw)
            cw = (jnp.dot(tri, hi, preferred_element_type=jnp.float32)
                  + jnp.dot(tri, mid, preferred_element_type=jnp.float32)
                  + jnp.dot(tri, lo, preferred_element_type=jnp.float32))
            tot = jnp.sum(lw, axis=0, keepdims=True)
            w_inv = jnp.exp(-cw)
            w_rem = jnp.exp(tot - cw)
            kd = k_ref[bi, rows, :] * (1.0 + (a_s - 1.0) * ka)
            ba = kk * a_s
            rt = r_ref[bi, rows, :] * jnp.exp(cw)
            at = -kk * jnp.exp(cw - lw)
            bt = ba * w_inv
            kt = kd * w_inv
            bp = ba * w_rem
            kp = kd * w_rem
            v = v_ref[bi, rows, :]
            wt = jnp.exp(tot)
            for sl in sls:
                g_mat.append(jnp.concatenate([at[:, sl], rt[:, sl]], axis=0).astype(bf))
                bk.append(jnp.concatenate([bt[:, sl], kt[:, sl]], axis=0).astype(bf))
                bkp.append(jnp.concatenate([bp[:, sl], kp[:, sl]], axis=0).astype(bf))
                vj.append(v[:, sl].astype(bf))
                w_tot.append(wt[:, sl])
        s_old = [s_scr[bi, j] for bi, j in chains]
        m12 = [lax.dot_general(g_mat[i], jnp.concatenate([bk[i], s_old[i].astype(bf)], axis=0), _NT,
                               preferred_element_type=jnp.float32) for i in ids]
        a_all = [jnp.where(strict2, m12[i][:L, :2 * L], 0.0) for i in ids]
        r_all = [jnp.where(incl2, m12[i][L:, :2 * L], 0.0).astype(bf) for i in ids]
        x0 = [m12[i][:L, 2 * L:] + _bdot(a_all[i][:, L:], vj[i]) for i in ids]
        w = [jnp.concatenate([a_all[i][:, :L], x0[i]], axis=1) for i in ids]
        keep_x = lax.broadcasted_iota(jnp.int32, (L, L + N), 1) >= L
        zpad = jnp.zeros((N, L + N), bf)
        for st in range(L.bit_length() - 1):
            wb = [w[i].astype(bf) for i in ids]
            w = [jnp.dot(wb[i], jnp.concatenate([wb[i], zpad], axis=0),
                         preferred_element_type=jnp.float32) + jnp.where(keep_x, w[i], 0.0)
                 for i in ids]
        uv = [jnp.concatenate([w[i][:, L:].astype(bf), vj[i]], axis=0) for i in ids]
        for i, (bi, j) in enumerate(chains):
            y_ref[bi, rows, sls[j]] = m12[i][L:, 2 * L:] + jnp.dot(r_all[i], uv[i],
                                                                  preferred_element_type=jnp.float32)
        for i, (bi, j) in enumerate(chains):
            s_scr[bi, j] = s_old[i] * w_tot[i] + lax.dot_general(
                uv[i], bkp[i], _TN, preferred_element_type=jnp.float32)
        return carry

    lax.fori_loop(0, n_chunks, chunk, 0)


def rwkv_scan(r, k, v, kk, lw, asig, k_a, *, batch, seq, heads_per_step=16, rows_per_step=256):
    d = r.shape[-1]
    N, L = RW_HEAD_SIZE, RW_CHUNK
    hw = _pick(d, heads_per_step * N)
    tb = _pick(seq, rows_per_step)
    assert tb % L == 0
    nt = seq // tb
    kern = functools.partial(_rwkv_scan_kernel, n_chunks=tb // L, n_heads=hw // N, n_batch=batch)

    def tmap(z, c):
        return c + z * (nt - 1 - 2 * c)

    shared = pl.BlockSpec((batch, tb, hw), lambda z, g, c: (0, tmap(z, c), g))
    per_dir = pl.BlockSpec((None, batch, tb, hw), lambda z, g, c: (z, 0, tmap(z, c), g))
    return pl.pallas_call(
        kern,
        grid=(2, d // hw, nt),
        in_specs=[shared, shared, shared, shared, per_dir, per_dir,
                  pl.BlockSpec((1, hw), lambda z, g, c: (0, g))],
        out_specs=per_dir,
        out_shape=jax.ShapeDtypeStruct((2, batch, seq, d), jnp.float32),
        scratch_shapes=[pltpu.VMEM((batch, hw // N, N, N), jnp.float32)],
        compiler_params=_params("parallel", "parallel", "arbitrary"),
        name="rwkv_scan",
    )(r, k, v, kk, lw, asig, k_a.reshape(1, d))


def _rwkv_post_kernel(y_ref, r_ref, k_ref, v_ref, as_ref, g_ref, lnw_ref, lnb_ref, ka_ref, rk_ref,
                      o_ref):
    y = y_ref[0] + y_ref[1]
    avg = 1.0 / RW_HEAD_SIZE
    mean = _head_sum(y, avg)
    yc = y - mean
    var = _head_sum(yc * yc, avg)
    yn = yc * lax.rsqrt(var + RW_LNX_EPS) * lnw_ref[...] + lnb_ref[...]
    k_sum = k_ref[...] * (2.0 + (as_ref[0] + as_ref[1] - 2.0) * ka_ref[...])
    bonus = _head_sum(r_ref[...] * k_sum * rk_ref[...], 1.0) * v_ref[...]
    o_ref[...] = ((yn + bonus) * g_ref[...]).astype(o_ref.dtype)


def rwkv_post(y, r, k, v, asig, g, ln_w, ln_b, k_a, r_k, *, tm=256, tc=512):
    n, d = r.shape
    tm = _pick(n, tm)
    tc = _pick(d, tc)
    tile = pl.BlockSpec((tm, tc), lambda i, j: (i, j))
    two = pl.BlockSpec((2, tm, tc), lambda i, j: (0, i, j))
    vec = pl.BlockSpec((1, tc), lambda i, j: (0, j))
    return pl.pallas_call(
        _rwkv_post_kernel,
        grid=(n // tm, d // tc),
        in_specs=[two, tile, tile, tile, two, tile, vec, vec, vec, vec],
        out_specs=tile,
        out_shape=jax.ShapeDtypeStruct((n, d), jnp.bfloat16),
        compiler_params=_params("parallel", "parallel"),
        name="rwkv_post",
    )(y, r, k, v, asig, g, ln_w.reshape(1, d), ln_b.reshape(1, d), k_a.reshape(1, d),
      r_k.reshape(1, d))


def rwkv_layer(x2, g_mix, mu, w_rkv, w0, w1, w2, a0, a1, a2, g1, g2, k_k, k_a, r_k, ln_w, ln_b,
               w_o, *, batch, seq):
    n, d = x2.shape
    bf = jnp.bfloat16
    xr, xk, xv, xw, xa, xg = rwkv_mix(x2, g_mix, mu, seq=seq)
    r = matmul(xr, w_rkv[0].astype(bf), name="rwkv_r")
    k = matmul(xk, w_rkv[1].astype(bf), name="rwkv_k")
    v = matmul(xv, w_rkv[2].astype(bf), name="rwkv_v")
    wl = matmul(xw, jnp.concatenate([w1[0], w1[1]], axis=1).astype(bf), epilogue="tanh",
                out_dtype=bf, name="rwkv_w1")
    al = matmul(xa, jnp.concatenate([a1[0], a1[1]], axis=1).astype(bf), out_dtype=bf,
                name="rwkv_a1")
    gpad = (-g1.shape[1]) % LANES
    gl = matmul(xg, jnp.pad(g1, ((0, 0), (0, gpad))).astype(bf), epilogue="sigmoid",
                out_dtype=bf, name="rwkv_g1")
    lw, asig, kk, g = rwkv_prep(k, wl, al, gl, w2.astype(bf), a2.astype(bf),
                                jnp.pad(g2, ((0, gpad), (0, 0))).astype(bf), w0, a0, k_k)
    shp = (batch, seq, d)
    y = rwkv_scan(r.reshape(shp), k.reshape(shp), v.reshape(shp), kk.reshape(shp),
                  lw.reshape((2,) + shp), asig.reshape((2,) + shp), k_a, batch=batch, seq=seq)
    o = rwkv_post(y.reshape(2, n, d), r, k, v, asig, g, ln_w, ln_b, k_a, r_k)
    return matmul(o, w_o.astype(bf), epilogue="residual", extra=x2, name="rwkv_out")


def kernel(x, norm_mix, norm_ffn, attn_w_qkv, attn_q_norm, attn_k_norm, attn_lambda, attn_sub_norm,
           attn_w_o, rwkv_mu, rwkv_w_rkv, rwkv_w0, rwkv_w1, rwkv_w2, rwkv_a0, rwkv_a1, rwkv_a2,
           rwkv_g1, rwkv_g2, rwkv_k_k, rwkv_k_a, rwkv_r_k, rwkv_ln_w, rwkv_ln_b, rwkv_w_o,
           peer_w_q, peer_sub_keys, peer_u, peer_v):
    batch, seq, d = x.shape
    depth = norm_mix.shape[0]
    x2 = x.reshape(batch * seq, d)
    for i in range(depth):
        j = i // N_MIXERS
        if i % N_MIXERS == 0:
            lambda_init = 0.8 - 0.6 * math.exp(-0.3 * i)
            x2 = differential_attention(x2, norm_mix[i], attn_w_qkv[j], attn_q_norm[j],
                                        attn_k_norm[j], attn_lambda[j], attn_sub_norm[j],
                                        attn_w_o[j], lambda_init, batch=batch, seq=seq)
        else:
            x2 = rwkv_layer(x2, norm_mix[i], rwkv_mu[j], rwkv_w_rkv[j], rwkv_w0[j], rwkv_w1[j],
                            rwkv_w2[j], rwkv_a0[j], rwkv_a1[j], rwkv_a2[j], rwkv_g1[j], rwkv_g2[j],
                            rwkv_k_k[j], rwkv_k_a[j], rwkv_r_k[j].reshape(-1), rwkv_ln_w[j],
                            rwkv_ln_b[j], rwkv_w_o[j], batch=batch, seq=seq)
        x2 = peer_layer(x2, norm_ffn[i], peer_w_q[i], peer_sub_keys[i], peer_u[i], peer_v[i])
    return x2.reshape(batch, seq, d)
```

```python
import functools
import math

import jax
import jax.numpy as jnp
from jax import lax
from jax.experimental import pallas as pl
from jax.experimental.pallas import tpu as pltpu

NORM_EPS = 1e-6
N_MIXERS = 2

DA_HEAD_DIM = 128
DA_V_DIM = 2 * DA_HEAD_DIM

RW_HEAD_SIZE = 64
RW_LNX_EPS = RW_HEAD_SIZE * 1e-5
RW_CHUNK = 64

PEER_HEADS = 8
PEER_NKEYS = 128
PEER_HALF = 128
PEER_TOPK = 16

V7X_VMEM_LIMIT_BYTES = 56 * 1024 * 1024
LANES = 128
SUBLANES = 8

_NT = (((1,), (1,)), ((), ()))
_TN = (((0,), (0,)), ((), ()))

_NEG_INF = float("-inf")


def _params(*sem):
    return pltpu.CompilerParams(dimension_semantics=sem,
                                vmem_limit_bytes=V7X_VMEM_LIMIT_BYTES)


def _pick(n, pref):
    if n <= pref:
        return n
    b = pref
    while n % b:
        b //= 2
    return b


def _rmsnorm_kernel(x_ref, g_ref, o_ref):
    x = x_ref[...]
    ms = jnp.mean(x * x, axis=-1, keepdims=True)
    o_ref[...] = (x * lax.rsqrt(ms + NORM_EPS) * g_ref[...]).astype(o_ref.dtype)


def rmsnorm_bf16(x, g):
    n, d = x.shape
    tm = _pick(n, 256)
    return pl.pallas_call(
        _rmsnorm_kernel,
        grid=(n // tm,),
        in_specs=[pl.BlockSpec((tm, d), lambda i: (i, 0)),
                  pl.BlockSpec((1, d), lambda i: (0, 0))],
        out_specs=pl.BlockSpec((tm, d), lambda i: (i, 0)),
        out_shape=jax.ShapeDtypeStruct((n, d), jnp.bfloat16),
        compiler_params=_params("parallel"),
        name="rmsnorm",
    )(x, g.reshape(1, d))


def _mm_kernel(*refs, epilogue, n_norm_tiles, nt):
    a_ref, b_ref = refs[0], refs[1]
    o_ref = refs[-1]
    if nt:
        acc = lax.dot_general(a_ref[...], b_ref[...], _NT, preferred_element_type=jnp.float32)
    else:
        acc = jnp.dot(a_ref[...], b_ref[...], preferred_element_type=jnp.float32)
    if epilogue == "plain":
        o_ref[...] = acc.astype(o_ref.dtype)
    elif epilogue == "residual":
        o_ref[...] = refs[2][...] + acc
    elif epilogue == "tanh":
        o_ref[...] = jnp.tanh(acc).astype(o_ref.dtype)
    elif epilogue == "sigmoid":
        o_ref[...] = jax.nn.sigmoid(acc).astype(o_ref.dtype)
    elif epilogue == "headnorm":
        gain_ref = refs[2]
        j = pl.program_id(1)

        @pl.when(j < n_norm_tiles)
        def _():
            for c in range(acc.shape[1] // DA_HEAD_DIM):
                sl = slice(c * DA_HEAD_DIM, (c + 1) * DA_HEAD_DIM)
                blk = acc[:, sl]
                ms = jnp.mean(blk * blk, axis=-1, keepdims=True)
                o_ref[:, sl] = (blk * lax.rsqrt(ms + NORM_EPS) * gain_ref[:, sl]).astype(o_ref.dtype)

        @pl.when(j >= n_norm_tiles)
        def _():
            o_ref[...] = acc.astype(o_ref.dtype)
    else:
        raise ValueError(epilogue)


def matmul(a, b, *, epilogue="plain", out_dtype=jnp.float32, extra=None,
           n_norm_cols=0, nt=False, n_cols=None, tm=1024, tn=512, name="matmul"):
    m, k = a.shape
    n, k2 = (b.shape if nt else b.shape[::-1])
    assert k == k2
    n = n if n_cols is None else n_cols
    tm = _pick(m, tm)
    tn = _pick(n, tn)
    in_specs = [pl.BlockSpec((tm, k), lambda i, j: (i, 0)),
                pl.BlockSpec((tn, k), lambda i, j: (j, 0)) if nt
                else pl.BlockSpec((k, tn), lambda i, j: (0, j))]
    args = [a, b]
    if epilogue == "residual":
        in_specs.append(pl.BlockSpec((tm, tn), lambda i, j: (i, j)))
        args.append(extra)
    elif epilogue == "headnorm":
        in_specs.append(pl.BlockSpec((1, tn), lambda i, j: (0, j)))
        args.append(extra.reshape(1, n))
        assert n_norm_cols % tn == 0
    kern = functools.partial(_mm_kernel, epilogue=epilogue,
                             n_norm_tiles=n_norm_cols // tn, nt=nt)
    return pl.pallas_call(
        kern,
        grid=(m // tm, n // tn),
        in_specs=in_specs,
        out_specs=pl.BlockSpec((tm, tn), lambda i, j: (i, j)),
        out_shape=jax.ShapeDtypeStruct((m, n), out_dtype),
        compiler_params=_params("parallel", "parallel"),
        name=name,
    )(*args)


ATTN_ROWS = 16


def _attn_kernel(par_ref, zero_ref, q_ref, k_ref, vt_ref, sub_ref, o_ref,
                 m_scr, l_scr, acc_scr, bias_scr, s0_scr, s1_scr, p0_scr, p1_scr,
                 *, tq, tk, tkb, out_scale):
    h = pl.program_id(1)
    qi = pl.program_id(2)
    kj = pl.program_id(3)
    d = DA_HEAD_DIM
    R = ATTN_ROWS
    n_chunk = tk // R
    n_kb = tkb // tk
    slope = par_ref[1 + h]

    @pl.when(kj == 0)
    def _():
        m_scr[...] = jnp.full(m_scr.shape, _NEG_INF, jnp.float32)
        l_scr[...] = jnp.zeros(l_scr.shape, jnp.float32)
        acc_scr[...] = jnp.zeros(acc_scr.shape, jnp.float32)
        for r in range(n_chunk):
            rows = slice(r * R, (r + 1) * R)
            rel = (lax.broadcasted_iota(jnp.int32, (R, tq), 0) + (r * R)
                   - lax.broadcasted_iota(jnp.int32, (R, tq), 1)).astype(jnp.float32)
            bias_scr[0, rows, :] = rel * (-slope)
            bias_scr[1, rows, :] = rel * slope
            bias_scr[2, rows, :] = jnp.abs(rel) * (-slope)

    tiles, shifts = [], []
    for kb in range(n_kb):
        off = kj * tkb + kb * tk - qi * tq
        tiles.append(jnp.where(off > 0, 0, jnp.where(off < 0, 1, 2)))
        shifts.append(jnp.abs(off).astype(jnp.float32) * (-slope))

    z0 = zero_ref[0]
    s_scr = (s0_scr, s1_scr)
    p_scr = (p0_scr, p1_scr)

    def rows_of(ref, start):
        return ref[pl.ds(pl.multiple_of(z0 + start, R), R), :]

    q = q_ref[...]

    def scores(c, kb):
        sv = lax.dot_general(k_ref[kb * tk:(kb + 1) * tk, c * d:(c + 1) * d], q[:, c * d:(c + 1) * d],
                             _NT, preferred_element_type=jnp.float32)
        s_scr[c][kb * tk:(kb + 1) * tk, :] = sv
        part = jnp.full((SUBLANES, tq), _NEG_INF, jnp.float32)
        for r in range(n_chunk):
            s = sv[r * R:(r + 1) * R, :] + bias_scr[tiles[kb], r * R:(r + 1) * R, :]
            part = jnp.maximum(part, jnp.max(s.reshape(R // SUBLANES, SUBLANES, tq), axis=0))
        return part + shifts[kb]

    blk_maxes = []
    for c in range(2):
        blk_max = scores(c, 0)
        for kb in range(1, n_kb):
            blk_max = jnp.maximum(blk_max, scores(c, kb))
        blk_maxes.append(blk_max)
    for c in range(2):
        m_prev = m_scr[c]
        m_new = jnp.maximum(m_prev, jnp.max(blk_maxes[c], axis=0, keepdims=True))
        alpha = jnp.exp2(m_prev - m_new)
        lsum = jnp.zeros((SUBLANES, tq), jnp.float32)
        for kb in range(n_kb):
            m_kb = m_new - shifts[kb]
            for r in range(n_chunk):
                p = jnp.exp2(rows_of(s_scr[c], kb * tk + r * R)
                             + bias_scr[tiles[kb], r * R:(r + 1) * R, :] - m_kb)
                p_scr[c][kb * tk + r * R:kb * tk + (r + 1) * R, :] = p.astype(jnp.bfloat16)
                lsum = lsum + jnp.sum(p.reshape(R // SUBLANES, SUBLANES, tq), axis=0)
        l_scr[c] = alpha * l_scr[c] + jnp.sum(lsum, axis=0, keepdims=True)
        acc_scr[c] = acc_scr[c] * alpha + jnp.dot(vt_ref[...], p_scr[c][...],
                                                  preferred_element_type=jnp.float32)
        m_scr[c] = m_new

    @pl.when(kj == pl.num_programs(3) - 1)
    def _():
        lam = par_ref[0]
        ot = acc_scr[0] * (1.0 / l_scr[0]) - acc_scr[1] * (lam / l_scr[1])
        ms = jnp.mean(ot * ot, axis=0, keepdims=True)
        o = (ot * lax.rsqrt(ms + NORM_EPS)).T
        o_ref[...] = (o * (sub_ref[...] * out_scale)).astype(o_ref.dtype)


def diff_attention_core(qk, vt, par, sub_norm, *, batch, seq, n_heads, out_scale,
                        tq=512, tk=512, tkb=4096):
    n = batch * seq
    tq = _pick(seq, tq)
    tkb = _pick(seq, tkb)
    tk = tq
    assert tkb % tk == 0 and tk % ATTN_ROWS == 0
    nq, nk = seq // tq, seq // tkb
    vd = DA_V_DIM
    kern = functools.partial(_attn_kernel, tq=tq, tk=tk, tkb=tkb, out_scale=out_scale)
    return pl.pallas_call(
        kern,
        grid=(batch, n_heads, nq, nk),
        in_specs=[
            pl.BlockSpec(memory_space=pltpu.SMEM),
            pl.BlockSpec(memory_space=pltpu.SMEM),
            pl.BlockSpec((tq, vd), lambda b, h, i, j: (b * nq + i, h)),
            pl.BlockSpec((tkb, vd), lambda b, h, i, j: (b * nk + j, n_heads + h)),
            pl.BlockSpec((vd, tkb), lambda b, h, i, j: (h, b * nk + j)),
            pl.BlockSpec((1, vd), lambda b, h, i, j: (0, 0)),
        ],
        out_specs=pl.BlockSpec((tq, vd), lambda b, h, i, j: (b * nq + i, h)),
        out_shape=jax.ShapeDtypeStruct((n, n_heads * vd), jnp.bfloat16),
        scratch_shapes=[pltpu.VMEM((2, 1, tq), jnp.float32),
                        pltpu.VMEM((2, 1, tq), jnp.float32),
                        pltpu.VMEM((2, vd, tq), jnp.float32),
                        pltpu.VMEM((3, tk, tq), jnp.float32),
                        pltpu.VMEM((tkb, tq), jnp.float32),
                        pltpu.VMEM((tkb, tq), jnp.float32),
                        pltpu.VMEM((tkb, tq), jnp.bfloat16),
                        pltpu.VMEM((tkb, tq), jnp.bfloat16)],
        compiler_params=_params("parallel", "parallel", "parallel", "arbitrary"),
        name="diff_attn",
    )(par, jnp.zeros((1,), jnp.int32), qk, qk, vt, sub_norm.reshape(1, vd))


def differential_attention(x2, g_mix, w_qkv, q_norm, k_norm, lam_p, sub_norm, w_o,
                           lambda_init, *, batch, seq):
    n, dm = x2.shape
    d = DA_HEAD_DIM
    n_heads = w_qkv.shape[1] // (3 * 2 * d)
    h = rmsnorm_bf16(x2, g_mix)
    qk_cols = 2 * n_heads * 2 * d
    log2e = math.log2(math.e)
    gains = jnp.concatenate([jnp.tile(q_norm * (d ** -0.5 * log2e), 2 * n_heads),
                             jnp.tile(k_norm, 2 * n_heads)])
    w_bf = w_qkv.astype(jnp.bfloat16)
    qk = matmul(h, w_bf, n_cols=qk_cols, epilogue="headnorm", extra=gains,
                n_norm_cols=qk_cols, out_dtype=jnp.bfloat16, name="attn_qk")
    vt = matmul(w_bf[:, qk_cols:].T, h, nt=True, out_dtype=jnp.bfloat16, name="attn_vt")
    lp = lam_p.astype(jnp.float32)
    lam = jnp.exp(jnp.sum(lp[0] * lp[1])) - jnp.exp(jnp.sum(lp[2] * lp[3])) + lambda_init
    slopes = 2.0 ** (-8.0 * jnp.arange(1, n_heads + 1, dtype=jnp.float32) / n_heads) * log2e
    par = jnp.concatenate([lam.reshape(1), slopes]).astype(jnp.float32)
    o = diff_attention_core(qk, vt, par, sub_norm, batch=batch, seq=seq, n_heads=n_heads,
                            out_scale=1.0 - lambda_init)
    return matmul(o, w_o.astype(jnp.bfloat16), epilogue="residual", extra=x2, name="attn_out")


def _top_values(s, k):
    rows = lax.broadcasted_iota(jnp.int32, s.shape, 0)
    n_rows = s.shape[0]
    vals = []
    for _ in range(k):
        m = jnp.max(s, axis=0, keepdims=True)
        vals.append(m)
        first = jnp.min(jnp.where(s == m, rows, n_rows), axis=0, keepdims=True)
        s = jnp.where(rows == first, _NEG_INF, s)
    return vals


def _peer_route_kernel(q_ref, keys_ref, s1_ref, e1_ref, s2_ref, e2_ref, tau_ref):
    kk = PEER_TOPK
    half = PEER_HALF
    taus = []
    for p in range(PEER_HEADS):
        qa = q_ref[:, (2 * p) * half:(2 * p + 1) * half]
        qb = q_ref[:, (2 * p + 1) * half:(2 * p + 2) * half]
        s1 = lax.dot_general(keys_ref[p, 0], qa, _NT, preferred_element_type=jnp.float32,
                             precision=lax.Precision.HIGHEST)
        s2 = lax.dot_general(keys_ref[p, 1], qb, _NT, preferred_element_type=jnp.float32,
                             precision=lax.Precision.HIGHEST)
        a = _top_values(s1, kk)
        b = _top_values(s2, kk)
        cands = [a[i] + b[j] for i in range(kk) for j in range(kk) if (i + 1) * (j + 1) <= kk]
        pad = (-len(cands)) % SUBLANES
        cands += [jnp.full_like(a[0], _NEG_INF)] * pad
        top = _top_values(jnp.concatenate(cands, axis=0), kk)
        mx = top[0]
        z = jnp.zeros_like(mx)
        for t in top:
            z = z + jnp.exp(t - mx)
        taus.append(top[kk - 1])
        s1_ref[p] = s1
        s2_ref[p] = s2
        e1_ref[p] = jnp.exp(s1 - a[0])
        e2_ref[p] = jnp.exp(s2 - b[0]) / z
    tau_ref[...] = jnp.concatenate(taus, axis=0)


def peer_route(q, sub_keys, *, tn=256):
    n = q.shape[0]
    tn = _pick(n, tn)
    ph, nk = PEER_HEADS, PEER_NKEYS
    big = jax.ShapeDtypeStruct((ph, nk, n), jnp.float32)
    bspec = pl.BlockSpec((ph, nk, tn), lambda i: (0, 0, i))
    return pl.pallas_call(
        _peer_route_kernel,
        grid=(n // tn,),
        in_specs=[pl.BlockSpec((tn, q.shape[1]), lambda i: (i, 0)),
                  pl.BlockSpec(sub_keys.shape, lambda i: (0, 0, 0, 0))],
        out_specs=[bspec, bspec, bspec, bspec, pl.BlockSpec((ph, tn), lambda i: (0, i))],
        out_shape=[big, big, big, big, jax.ShapeDtypeStruct((ph, n), jnp.float32)],
        compiler_params=_params("parallel"),
        name="peer_route",
    )(q, sub_keys)


def _peer_dense_kernel(h_ref, u_ref, v_ref, x_ref, s1_ref, e1_ref, s2_ref, e2_ref, tau_ref,
                       o_ref, *, a_per_tile, a_rows):
    e = pl.program_id(1)

    @pl.when(e == 0)
    def _():
        o_ref[...] = x_ref[...]

    act = lax.dot_general(u_ref[...], h_ref[...], _NT, preferred_element_type=jnp.float32)
    act = 0.5 * act * (1.0 + lax.erf(act * (2.0 ** -0.5)))
    nk = PEER_NKEYS
    a_base = lax.rem(e * a_per_tile, a_rows)
    parts = []
    for ai in range(a_per_tile):
        a = a_base + ai
        w = None
        for p in range(PEER_HEADS):
            s1row = s1_ref[p, pl.ds(a, 1), :]
            e1row = e1_ref[p, pl.ds(a, 1), :]
            taurow = tau_ref[pl.ds(p, 1), :]
            sel = jnp.where(s1row + s2_ref[p] >= taurow, e2_ref[p], 0.0) * e1row
            w = sel if w is None else w + sel
        parts.append(act[ai * nk:(ai + 1) * nk, :] * w)
    pt = jnp.concatenate(parts, axis=0)
    o_ref[...] += jnp.dot(pt.T.astype(jnp.bfloat16), v_ref[...], preferred_element_type=jnp.float32)


def peer_dense(hf, u, v, x2, s1, e1, s2, e2, tau, *, tn=512, te=512):
    n, d = hf.shape
    n_exp = u.shape[0]
    tn = _pick(n, tn)
    te = _pick(n_exp, te)
    nk, ph = PEER_NKEYS, PEER_HEADS
    a_per_tile = te // nk
    a_rows = max(a_per_tile, SUBLANES)
    assert a_rows % a_per_tile == 0
    kern = functools.partial(_peer_dense_kernel, a_per_tile=a_per_tile, a_rows=a_rows)
    tok = lambda i, e: (i, 0)
    once = pl.Buffered(1)
    first = pl.BlockSpec((ph, a_rows, tn), lambda i, e: (0, (e * a_per_tile) // a_rows, i))
    second = pl.BlockSpec((ph, nk, tn), lambda i, e: (0, 0, i), pipeline_mode=once)
    return pl.pallas_call(
        kern,
        grid=(n // tn, n_exp // te),
        in_specs=[pl.BlockSpec((tn, d), tok, pipeline_mode=once),
                  pl.BlockSpec((te, d), lambda i, e: (e, 0)),
                  pl.BlockSpec((te, d), lambda i, e: (e, 0)),
                  pl.BlockSpec((tn, d), tok, pipeline_mode=once),
                  first, first, second, second,
                  pl.BlockSpec((ph, tn), lambda i, e: (0, i))],
        out_specs=pl.BlockSpec((tn, d), tok, pipeline_mode=once),
        out_shape=jax.ShapeDtypeStruct((n, d), jnp.float32),
        compiler_params=_params("parallel", "arbitrary"),
        name="peer_dense",
    )(hf, u, v, x2, s1, e1, s2, e2, tau)


def peer_layer(x2, g_ffn, w_q, sub_keys, u_tab, v_tab):
    hf = rmsnorm_bf16(x2, g_ffn)
    q = matmul(hf, w_q.astype(jnp.bfloat16), name="peer_q")
    s1, e1, s2, e2, tau = peer_route(q, sub_keys)
    return peer_dense(hf, u_tab.astype(jnp.bfloat16), v_tab.astype(jnp.bfloat16), x2,
                      s1, e1, s2, e2, tau)


def _rwkv_mix_kernel(x_ref, xp_ref, xn_ref, g_ref, mu_ref, *o_refs, tm, seq):
    i = pl.program_id(0)
    g = g_ref[...]

    def norm(x):
        ms = jnp.mean(x * x, axis=-1, keepdims=True)
        return x * lax.rsqrt(ms + NORM_EPS) * g

    h = norm(x_ref[...])
    hp = norm(xp_ref[...])[SUBLANES - 1:SUBLANES, :]
    hn = norm(xn_ref[...])[0:1, :]
    t0 = lax.rem(i * tm, seq)
    prev_row = jnp.where(t0 == 0, 0.0, hp)
    next_row = jnp.where(t0 + tm == seq, 0.0, hn)
    rows = lax.broadcasted_iota(jnp.int32, h.shape, 0)
    h_prev = jnp.where(rows == 0, prev_row, pltpu.roll(h, 1, axis=0))
    h_next = jnp.where(rows == tm - 1, next_row, pltpu.roll(h, tm - 1, axis=0))
    xx = 0.5 * (h_prev + h_next) - h
    for idx, o_ref in enumerate(o_refs):
        o_ref[...] = (h + xx * mu_ref[idx:idx + 1, :]).astype(o_ref.dtype)


def rwkv_mix(x2, g_mix, mu, *, seq, tm=256):
    n, d = x2.shape
    tm = _pick(seq, tm)
    assert tm % SUBLANES == 0 and seq % tm == 0
    r8 = tm // SUBLANES
    last = n // SUBLANES - 1
    kern = functools.partial(_rwkv_mix_kernel, tm=tm, seq=seq)
    out = jax.ShapeDtypeStruct((n, d), jnp.bfloat16)
    return pl.pallas_call(
        kern,
        grid=(n // tm,),
        in_specs=[pl.BlockSpec((tm, d), lambda i: (i, 0)),
                  pl.BlockSpec((SUBLANES, d), lambda i: (jnp.maximum(i * r8 - 1, 0), 0)),
                  pl.BlockSpec((SUBLANES, d), lambda i: (jnp.minimum((i + 1) * r8, last), 0)),
                  pl.BlockSpec((1, d), lambda i: (0, 0)),
                  pl.BlockSpec((SUBLANES, d), lambda i: (0, 0))],
        out_specs=[pl.BlockSpec((tm, d), lambda i: (i, 0))] * 6,
        out_shape=[out] * 6,
        compiler_params=_params("parallel"),
        name="rwkv_mix",
    )(x2, x2, x2, g_mix.reshape(1, d), jnp.pad(mu, ((0, SUBLANES - mu.shape[0]), (0, 0))))


HEAD_SUM_WIDTH = 256


def _split3(x):
    hi = x.astype(jnp.bfloat16)
    r1 = x - hi.astype(jnp.float32)
    mid = r1.astype(jnp.bfloat16)
    lo = (r1 - mid.astype(jnp.float32)).astype(jnp.bfloat16)
    return hi, mid, lo


def _head_sum(x, scale):
    w = min(HEAD_SUM_WIDTH, x.shape[1])
    r = lax.broadcasted_iota(jnp.int32, (w, w), 0) // RW_HEAD_SIZE
    c = lax.broadcasted_iota(jnp.int32, (w, w), 1) // RW_HEAD_SIZE
    ones = jnp.where(r == c, scale, 0.0).astype(jnp.bfloat16)
    out = []
    for g in range(x.shape[1] // w):
        terms = _split3(x[:, g * w:(g + 1) * w])
        out.append(sum(jnp.dot(t, ones, preferred_element_type=jnp.float32) for t in terms))
    return out[0] if len(out) == 1 else jnp.concatenate(out, axis=1)


def _rwkv_prep_kernel(k_ref, wl_ref, al_ref, gl_ref, w2_ref, a2_ref, g2_ref, w0_ref, a0_ref,
                      kkw_ref, lw_ref, as_ref, kk_ref, g_ref, *, lora):
    for z in range(2):
        sl = slice(z * lora, (z + 1) * lora)
        wz = w0_ref[z:z + 1, :] + jnp.dot(wl_ref[:, sl], w2_ref[z], preferred_element_type=jnp.float32)
        y = -wz
        softplus = jnp.maximum(y, 0.0) + jnp.log1p(jnp.exp(-jnp.abs(y)))
        lw_ref[z] = -jnp.exp(-softplus - 0.5)
        az = a0_ref[z:z + 1, :] + jnp.dot(al_ref[:, sl], a2_ref[z], preferred_element_type=jnp.float32)
        as_ref[z] = jax.nn.sigmoid(az)
    g_ref[...] = jnp.dot(gl_ref[...], g2_ref[...], preferred_element_type=jnp.float32)
    kr = k_ref[...] * kkw_ref[...]
    ss = _head_sum(kr * kr, 1.0)
    kk_ref[...] = kr / jnp.maximum(jnp.sqrt(ss), 1e-12)


def rwkv_prep(k, wl, al, gl, w2, a2, g2, w0, a0, k_k, *, tm=512, tc=512):
    n, d = k.shape
    tm = _pick(n, tm)
    tc = _pick(d, tc)
    lora = w2.shape[1]
    glw = gl.shape[1]
    kern = functools.partial(_rwkv_prep_kernel, lora=lora)
    row = lambda i, j: (i, 0)
    tile = lambda i, j: (i, j)
    two = jax.ShapeDtypeStruct((2, n, d), jnp.float32)
    one = jax.ShapeDtypeStruct((n, d), jnp.float32)
    return pl.pallas_call(
        kern,
        grid=(n // tm, d // tc),
        in_specs=[pl.BlockSpec((tm, tc), tile),
                  pl.BlockSpec((tm, 2 * lora), row),
                  pl.BlockSpec((tm, 2 * lora), row),
                  pl.BlockSpec((tm, glw), row),
                  pl.BlockSpec((2, lora, tc), lambda i, j: (0, 0, j)),
                  pl.BlockSpec((2, lora, tc), lambda i, j: (0, 0, j)),
                  pl.BlockSpec((glw, tc), lambda i, j: (0, j)),
                  pl.BlockSpec((2, tc), lambda i, j: (0, j)),
                  pl.BlockSpec((2, tc), lambda i, j: (0, j)),
                  pl.BlockSpec((1, tc), lambda i, j: (0, j))],
        out_specs=[pl.BlockSpec((2, tm, tc), lambda i, j: (0, i, j)),
                   pl.BlockSpec((2, tm, tc), lambda i, j: (0, i, j)),
                   pl.BlockSpec((tm, tc), tile),
                   pl.BlockSpec((tm, tc), tile)],
        out_shape=[two, two, one, one],
        compiler_params=_params("parallel", "parallel"),
        name="rwkv_prep",
    )(k, wl, al, gl, w2, a2, g2, w0, a0, k_k.reshape(1, d))


def _bdot(a, b):
    return jnp.dot(a.astype(jnp.bfloat16), b.astype(jnp.bfloat16),
                   preferred_element_type=jnp.float32)


def _rwkv_scan_kernel(r_ref, k_ref, v_ref, kk_ref, lw_ref, as_ref, ka_ref, y_ref, s_scr,
                      *, n_chunks, n_heads, n_batch):
    z = pl.program_id(0)
    c = pl.program_id(2)
    L = RW_CHUNK
    N = RW_HEAD_SIZE
    bf = jnp.bfloat16

    @pl.when(c == 0)
    def _():
        s_scr[...] = jnp.zeros(s_scr.shape, jnp.float32)

    sgn = 1 - 2 * z
    row = lax.broadcasted_iota(jnp.int32, (L, 2 * L), 0)
    col = lax.broadcasted_iota(jnp.int32, (L, 2 * L), 1)
    col = jnp.where(col >= L, col - L, col)
    delta = (col - row) * sgn
    strict2 = delta < 0
    incl2 = delta <= 0
    tri = jnp.where(incl2[:, :L], 1.0, 0.0).astype(bf)
    ka = ka_ref[...]
    sls = [slice(j * N, (j + 1) * N) for j in range(n_heads)]
    chains = [(bi, j) for bi in range(n_batch) for j in range(n_heads)]
    ids = range(len(chains))

    def chunk(ci, carry):
        cc = jnp.where(z == 0, ci, n_chunks - 1 - ci)
        rows = pl.ds(pl.multiple_of(cc * L, L), L)
        g_mat, bk, bkp, vj, w_tot = [], [], [], [], []
        for bi in range(n_batch):
            lw = lw_ref[bi, rows, :]
            a_s = as_ref[bi, rows, :]
            kk = kk_ref[bi, rows, :]
            hi, mid, lo = _split3(lw)
            cw = (jnp.dot(tri, hi, preferred_element_type=jnp.float32)
                  + jnp.dot(tri, mid, preferred_element_type=jnp.float32)
                  + jnp.dot(tri, lo, preferred_element_type=jnp.float32))
            tot = jnp.sum(lw, axis=0, keepdims=True)
            w_inv = jnp.exp(-cw)
            w_rem = jnp.exp(tot - cw)
            kd = k_ref[bi, rows, :] * (1.0 + (a_s - 1.0) * ka)
            ba = kk * a_s
            rt = r_ref[bi, rows, :] * jnp.exp(cw)
            at = -kk * jnp.exp(cw - lw)
            bt = ba * w_inv
            kt = kd * w_inv
            bp = ba * w_rem
            kp = kd * w_rem
            v = v_ref[bi, rows, :]
            wt = jnp.exp(tot)
            for sl in sls:
                g_mat.append(jnp.concatenate([at[:, sl], rt[:, sl]], axis=0).astype(bf))
                bk.append(jnp.concatenate([bt[:, sl], kt[:, sl]], axis=0).astype(bf))
                bkp.append(jnp.concatenate([bp[:, sl], kp[:, sl]], axis=0).astype(bf))
                vj.append(v[:, sl].astype(bf))
                w_tot.append(wt[:, sl])
        s_old = [s_scr[bi, j] for bi, j in chains]
        m12 = [lax.dot_general(g_mat[i], jnp.concatenate([bk[i], s_old[i].astype(bf)], axis=0), _NT,
                               preferred_element_type=jnp.float32) for i in ids]
        a_all = [jnp.where(strict2, m12[i][:L, :2 * L], 0.0) for i in ids]
        r_all = [jnp.where(incl2, m12[i][L:, :2 * L], 0.0).astype(bf) for i in ids]
        x0 = [m12[i][:L, 2 * L:] + _bdot(a_all[i][:, L:], vj[i]) for i in ids]
        w = [jnp.concatenate([a_all[i][:, :L], x0[i]], axis=1) for i in ids]
        keep_x = lax.broadcasted_iota(jnp.int32, (L, L + N), 1) >= L
        zpad = jnp.zeros((N, L + N), bf)
        for st in range(L.bit_length() - 1):
            wb = [w[i].astype(bf) for i in ids]
            w = [jnp.dot(wb[i], jnp.concatenate([wb[i], zpad], axis=0),
                         preferred_element_type=jnp.float32) + jnp.where(keep_x, w[i], 0.0)
                 for i in ids]
        uv = [jnp.concatenate([w[i][:, L:].astype(bf), vj[i]], axis=0) for i in ids]
        for i, (bi, j) in enumerate(chains):
            y_ref[bi, rows, sls[j]] = m12[i][L:, 2 * L:] + jnp.dot(r_all[i], uv[i],
                                                                  preferred_element_type=jnp.float32)
        for i, (bi, j) in enumerate(chains):
            s_scr[bi, j] = s_old[i] * w_tot[i] + lax.dot_general(
                uv[i], bkp[i], _TN, preferred_element_type=jnp.float32)
        return carry

    lax.fori_loop(0, n_chunks, chunk, 0)


def rwkv_scan(r, k, v, kk, lw, asig, k_a, *, batch, seq, heads_per_step=16, rows_per_step=256):
    d = r.shape[-1]
    N, L = RW_HEAD_SIZE, RW_CHUNK
    hw = _pick(d, heads_per_step * N)
    tb = _pick(seq, rows_per_step)
    assert tb % L == 0
    nt = seq // tb
    kern = functools.partial(_rwkv_scan_kernel, n_chunks=tb // L, n_heads=hw // N, n_batch=batch)

    def tmap(z, c):
        return c + z * (nt - 1 - 2 * c)

    shared = pl.BlockSpec((batch, tb, hw), lambda z, g, c: (0, tmap(z, c), g))
    per_dir = pl.BlockSpec((None, batch, tb, hw), lambda z, g, c: (z, 0, tmap(z, c), g))
    return pl.pallas_call(
        kern,
        grid=(2, d // hw, nt),
        in_specs=[shared, shared, shared, shared, per_dir, per_dir,
                  pl.BlockSpec((1, hw), lambda z, g, c: (0, g))],
        out_specs=per_dir,
        out_shape=jax.ShapeDtypeStruct((2, batch, seq, d), jnp.float32),
        scratch_shapes=[pltpu.VMEM((batch, hw // N, N, N), jnp.float32)],
        compiler_params=_params("parallel", "parallel", "arbitrary"),
        name="rwkv_scan",
    )(r, k, v, kk, lw, asig, k_a.reshape(1, d))


def _rwkv_post_kernel(y_ref, r_ref, k_ref, v_ref, as_ref, g_ref, lnw_ref, lnb_ref, ka_ref, rk_ref,
                      o_ref):
    y = y_ref[0] + y_ref[1]
    avg = 1.0 / RW_HEAD_SIZE
    mean = _head_sum(y, avg)
    yc = y - mean
    var = _head_sum(yc * yc, avg)
    yn = yc * lax.rsqrt(var + RW_LNX_EPS) * lnw_ref[...] + lnb_ref[...]
    k_sum = k_ref[...] * (2.0 + (as_ref[0] + as_ref[1] - 2.0) * ka_ref[...])
    bonus = _head_sum(r_ref[...] * k_sum * rk_ref[...], 1.0) * v_ref[...]
    o_ref[...] = ((yn + bonus) * g_ref[...]).astype(o_ref.dtype)


def rwkv_post(y, r, k, v, asig, g, ln_w, ln_b, k_a, r_k, *, tm=256, tc=512):
    n, d = r.shape
    tm = _pick(n, tm)
    tc = _pick(d, tc)
    tile = pl.BlockSpec((tm, tc), lambda i, j: (i, j))
    two = pl.BlockSpec((2, tm, tc), lambda i, j: (0, i, j))
    vec = pl.BlockSpec((1, tc), lambda i, j: (0, j))
    return pl.pallas_call(
        _rwkv_post_kernel,
        grid=(n // tm, d // tc),
        in_specs=[two, tile, tile, tile, two, tile, vec, vec, vec, vec],
        out_specs=tile,
        out_shape=jax.ShapeDtypeStruct((n, d), jnp.bfloat16),
        compiler_params=_params("parallel", "parallel"),
        name="rwkv_post",
    )(y, r, k, v, asig, g, ln_w.reshape(1, d), ln_b.reshape(1, d), k_a.reshape(1, d),
      r_k.reshape(1, d))


def rwkv_layer(x2, g_mix, mu, w_rkv, w0, w1, w2, a0, a1, a2, g1, g2, k_k, k_a, r_k, ln_w, ln_b,
               w_o, *, batch, seq):
    n, d = x2.shape
    bf = jnp.bfloat16
    xr, xk, xv, xw, xa, xg = rwkv_mix(x2, g_mix, mu, seq=seq)
    r = matmul(xr, w_rkv[0].astype(bf), name="rwkv_r")
    k = matmul(xk, w_rkv[1].astype(bf), name="rwkv_k")
    v = matmul(xv, w_rkv[2].astype(bf), name="rwkv_v")
    wl = matmul(xw, jnp.concatenate([w1[0], w1[1]], axis=1).astype(bf), epilogue="tanh",
                out_dtype=bf, name="rwkv_w1")
    al = matmul(xa, jnp.concatenate([a1[0], a1[1]], axis=1).astype(bf), out_dtype=bf,
                name="rwkv_a1")
    gpad = (-g1.shape[1]) % LANES
    gl = matmul(xg, jnp.pad(g1, ((0, 0), (0, gpad))).astype(bf), epilogue="sigmoid",
                out_dtype=bf, name="rwkv_g1")
    lw, asig, kk, g = rwkv_prep(k, wl, al, gl, w2.astype(bf), a2.astype(bf),
                                jnp.pad(g2, ((0, gpad), (0, 0))).astype(bf), w0, a0, k_k)
    shp = (batch, seq, d)
    y = rwkv_scan(r.reshape(shp), k.reshape(shp), v.reshape(shp), kk.reshape(shp),
                  lw.reshape((2,) + shp), asig.reshape((2,) + shp), k_a, batch=batch, seq=seq)
    o = rwkv_post(y.reshape(2, n, d), r, k, v, asig, g, ln_w, ln_b, k_a, r_k)
    return matmul(o, w_o.astype(bf), epilogue="residual", extra=x2, name="rwkv_out")


def kernel(x, norm_mix, norm_ffn, attn_w_qkv, attn_q_norm, attn_k_norm, attn_lambda, attn_sub_norm,
           attn_w_o, rwkv_mu, rwkv_w_rkv, rwkv_w0, rwkv_w1, rwkv_w2, rwkv_a0, rwkv_a1, rwkv_a2,
           rwkv_g1, rwkv_g2, rwkv_k_k, rwkv_k_a, rwkv_r_k, rwkv_ln_w, rwkv_ln_b, rwkv_w_o,
           peer_w_q, peer_sub_keys, peer_u, peer_v):
    batch, seq, d = x.shape
    depth = norm_mix.shape[0]
    x2 = x.reshape(batch * seq, d)
    for i in range(depth):
        j = i // N_MIXERS
        if i % N_MIXERS == 0:
            lambda_init = 0.8 - 0.6 * math.exp(-0.3 * i)
            x2 = differential_attention(x2, norm_mix[i], attn_w_qkv[j], attn_q_norm[j],
                                        attn_k_norm[j], attn_lambda[j], attn_sub_norm[j],
                                        attn_w_o[j], lambda_init, batch=batch, seq=seq)
        else:
            x2 = rwkv_layer(x2, norm_mix[i], rwkv_mu[j], rwkv_w_rkv[j], rwkv_w0[j], rwkv_w1[j],
                            rwkv_w2[j], rwkv_a0[j], rwkv_a1[j], rwkv_a2[j], rwkv_g1[j], rwkv_g2[j],
                            rwkv_k_k[j], rwkv_k_a[j], rwkv_r_k[j].reshape(-1), rwkv_ln_w[j],
                            rwkv_ln_b[j], rwkv_w_o[j], batch=batch, seq=seq)
        x2 = peer_layer(x2, norm_ffn[i], peer_w_q[i], peer_sub_keys[i], peer_u[i], peer_v[i])
    return x2.reshape(batch, seq, d)
```

```python
import functools
import math

import jax
import jax.numpy as jnp
from jax import lax
from jax.experimental import pallas as pl
from jax.experimental.pallas import tpu as pltpu

NORM_EPS = 1e-6
N_MIXERS = 2

DA_HEAD_DIM = 128
DA_V_DIM = 2 * DA_HEAD_DIM

RW_HEAD_SIZE = 64
RW_LNX_EPS = RW_HEAD_SIZE * 1e-5
RW_CHUNK = 64

PEER_HEADS = 8
PEER_NKEYS = 128
PEER_HALF = 128
PEER_TOPK = 16

V7X_VMEM_LIMIT_BYTES = 56 * 1024 * 1024
LANES = 128
SUBLANES = 8

_NT = (((1,), (1,)), ((), ()))
_TN = (((0,), (0,)), ((), ()))

_NEG_INF = float("-inf")


def _params(*sem):
    return pltpu.CompilerParams(dimension_semantics=sem,
                                vmem_limit_bytes=V7X_VMEM_LIMIT_BYTES)


def _pick(n, pref):
    if n <= pref:
        return n
    b = pref
    while n % b:
        b //= 2
    return b


def _rmsnorm_kernel(x_ref, g_ref, o_ref):
    x = x_ref[...]
    ms = jnp.mean(x * x, axis=-1, keepdims=True)
    o_ref[...] = (x * lax.rsqrt(ms + NORM_EPS) * g_ref[...]).astype(o_ref.dtype)


def rmsnorm_bf16(x, g):
    n, d = x.shape
    tm = _pick(n, 256)
    return pl.pallas_call(
        _rmsnorm_kernel,
        grid=(n // tm,),
        in_specs=[pl.BlockSpec((tm, d), lambda i: (i, 0)),
                  pl.BlockSpec((1, d), lambda i: (0, 0))],
        out_specs=pl.BlockSpec((tm, d), lambda i: (i, 0)),
        out_shape=jax.ShapeDtypeStruct((n, d), jnp.bfloat16),
        compiler_params=_params("parallel"),
        name="rmsnorm",
    )(x, g.reshape(1, d))


def _mm_kernel(*refs, epilogue, n_norm_tiles, nt):
    a_ref, b_ref = refs[0], refs[1]
    o_ref = refs[-1]
    if nt:
        acc = lax.dot_general(a_ref[...], b_ref[...], _NT, preferred_element_type=jnp.float32)
    else:
        acc = jnp.dot(a_ref[...], b_ref[...], preferred_element_type=jnp.float32)
    if epilogue == "plain":
        o_ref[...] = acc.astype(o_ref.dtype)
    elif epilogue == "residual":
        o_ref[...] = refs[2][...] + acc
    elif epilogue == "tanh":
        o_ref[...] = jnp.tanh(acc).astype(o_ref.dtype)
    elif epilogue == "sigmoid":
        o_ref[...] = jax.nn.sigmoid(acc).astype(o_ref.dtype)
    elif epilogue == "headnorm":
        gain_ref = refs[2]
        j = pl.program_id(1)

        @pl.when(j < n_norm_tiles)
        def _():
            for c in range(acc.shape[1] // DA_HEAD_DIM):
                sl = slice(c * DA_HEAD_DIM, (c + 1) * DA_HEAD_DIM)
                blk = acc[:, sl]
                ms = jnp.mean(blk * blk, axis=-1, keepdims=True)
                o_ref[:, sl] = (blk * lax.rsqrt(ms + NORM_EPS) * gain_ref[:, sl]).astype(o_ref.dtype)

        @pl.when(j >= n_norm_tiles)
        def _():
            o_ref[...] = acc.astype(o_ref.dtype)
    else:
        raise ValueError(epilogue)


def matmul(a, b, *, epilogue="plain", out_dtype=jnp.float32, extra=None,
           n_norm_cols=0, nt=False, n_cols=None, tm=1024, tn=512, name="matmul"):
    m, k = a.shape
    n, k2 = (b.shape if nt else b.shape[::-1])
    assert k == k2
    n = n if n_cols is None else n_cols
    tm = _pick(m, tm)
    tn = _pick(n, tn)
    in_specs = [pl.BlockSpec((tm, k), lambda i, j: (i, 0)),
                pl.BlockSpec((tn, k), lambda i, j: (j, 0)) if nt
                else pl.BlockSpec((k, tn), lambda i, j: (0, j))]
    args = [a, b]
    if epilogue == "residual":
        in_specs.append(pl.BlockSpec((tm, tn), lambda i, j: (i, j)))
        args.append(extra)
    elif epilogue == "headnorm":
        in_specs.append(pl.BlockSpec((1, tn), lambda i, j: (0, j)))
        args.append(extra.reshape(1, n))
        assert n_norm_cols % tn == 0
    kern = functools.partial(_mm_kernel, epilogue=epilogue,
                             n_norm_tiles=n_norm_cols // tn, nt=nt)
    return pl.pallas_call(
        kern,
        grid=(m // tm, n // tn),
        in_specs=in_specs,
        out_specs=pl.BlockSpec((tm, tn), lambda i, j: (i, j)),
        out_shape=jax.ShapeDtypeStruct((m, n), out_dtype),
        compiler_params=_params("parallel", "parallel"),
        name=name,
    )(*args)


ATTN_ROWS = 16


def _attn_kernel(par_ref, zero_ref, q_ref, k_ref, vt_ref, sub_ref, o_ref,
                 m_scr, l_scr, acc_scr, bias_scr, s0_scr, s1_scr, p0_scr, p1_scr,
                 *, tq, tk, tkb, out_scale):
    h = pl.program_id(1)
    qi = pl.program_id(2)
    kj = pl.program_id(3)
    d = DA_HEAD_DIM
    R = ATTN_ROWS
    n_chunk = tk // R
    n_kb = tkb // tk
    slope = par_ref[1 + h]

    @pl.when(kj == 0)
    def _():
        m_scr[...] = jnp.full(m_scr.shape, _NEG_INF, jnp.float32)
        l_scr[...] = jnp.zeros(l_scr.shape, jnp.float32)
        acc_scr[...] = jnp.zeros(acc_scr.shape, jnp.float32)

    @pl.when((kj == 0) & (qi == 0))
    def _():
        for r in range(n_chunk):
            rows = slice(r * R, (r + 1) * R)
            rel =(lax.broadcasted_iota(jnp.int32, (R, tq), 0) + (r * R)
                   - lax.broadcasted_iota(jnp.int32, (R, tq), 1)).astype(jnp.float32)
            bias_scr[0, rows, :] = rel * (-slope)
            bias_scr[1, rows, :] = rel * slope
            bias_scr[2, rows, :] = jnp.abs(rel) * (-slope)

    tiles, shifts = [], []
    for kb in range(n_kb):
        off = kj * tkb + kb * tk - qi * tq
        tiles.append(jnp.where(off > 0, 0, jnp.where(off < 0, 1, 2)))
        shifts.append(jnp.abs(off).astype(jnp.float32) * (-slope))

    z0 = zero_ref[0]
    s_scr = (s0_scr, s1_scr)
    p_scr = (p0_scr, p1_scr)

    def rows_of(ref, start):
        return ref[pl.ds(pl.multiple_of(z0 + start, R), R), :]

    q = q_ref[...]

    def scores(c, kb):
        sv = lax.dot_general(k_ref[kb * tk:(kb + 1) * tk, c * d:(c + 1) * d], q[:, c * d:(c + 1) * d],
                             _NT, preferred_element_type=jnp.float32)
        part = jnp.full((SUBLANES, tq), _NEG_INF, jnp.float32)
        for r in range(n_chunk):
            s = sv[r * R:(r + 1) * R, :] + bias_scr[tiles[kb], r * R:(r + 1) * R, :]
            s_scr[c][kb * tk + r * R:kb * tk + (r + 1) * R, :] = s
            part =jnp.maximum(part, jnp.max(s.reshape(R // SUBLANES, SUBLANES, tq), axis=0))
        return part + shifts[kb]

    blk_maxes = []
    for c in range(2):
        blk_max = scores(c, 0)
        for kb in range(1, n_kb):
            blk_max = jnp.maximum(blk_max, scores(c, kb))
        blk_maxes.append(blk_max)
    for c in range(2):
        m_prev = m_scr[c]
        m_new = jnp.maximum(m_prev, jnp.max(blk_maxes[c], axis=0, keepdims=True))
        alpha = jnp.exp2(m_prev - m_new)
        lsum = jnp.zeros((SUBLANES, tq), jnp.float32)
        for kb in range(n_kb):
            m_kb = m_new - shifts[kb]
            for r in range(n_chunk):
                p = jnp.exp2(rows_of(s_scr[c], kb * tk + r * R) - m_kb)
                p_scr[c][kb * tk + r * R:kb * tk + (r + 1) * R, :] = p.astype(jnp.bfloat16)
                lsum = lsum + jnp.sum(p.reshape(R // SUBLANES, SUBLANES, tq), axis=0)
        l_scr[c] = alpha * l_scr[c] + jnp.sum(lsum, axis=0, keepdims=True)
        acc_scr[c] = acc_scr[c] * alpha + jnp.dot(vt_ref[...], p_scr[c][...],
                                                  preferred_element_type=jnp.float32)
        m_scr[c] = m_new

    @pl.when(kj == pl.num_programs(3) - 1)
    def _():
        lam = par_ref[0]
        ot = acc_scr[0] * (1.0 / l_scr[0]) - acc_scr[1] * (lam / l_scr[1])
        ms = jnp.mean(ot * ot, axis=0, keepdims=True)
        o = (ot * lax.rsqrt(ms + NORM_EPS)).T
        o_ref[...] = (o * (sub_ref[...] * out_scale)).astype(o_ref.dtype)


def diff_attention_core(qk, vt, par, sub_norm, *, batch, seq, n_heads, out_scale,
                        tq=512, tk=512, tkb=2048):
    n = batch * seq
    tq = _pick(seq, tq)
    tkb = _pick(seq, tkb)
    tk = tq
    assert tkb % tk == 0 and tk % ATTN_ROWS == 0
    nq, nk = seq // tq, seq // tkb
    vd = DA_V_DIM
    kern = functools.partial(_attn_kernel, tq=tq, tk=tk, tkb=tkb, out_scale=out_scale)
    return pl.pallas_call(
        kern,
        grid=(batch, n_heads, nq, nk),
        in_specs=[
            pl.BlockSpec(memory_space=pltpu.SMEM),
            pl.BlockSpec(memory_space=pltpu.SMEM),
            pl.BlockSpec((tq, vd), lambda b, h, i, j: (b * nq + i, h)),
            pl.BlockSpec((tkb, vd), lambda b, h, i, j: (b * nk + j, n_heads + h)),
            pl.BlockSpec((vd, tkb), lambda b, h, i, j: (h, b * nk + j)),
            pl.BlockSpec((1, vd), lambda b, h, i, j: (0, 0)),
        ],
        out_specs=pl.BlockSpec((tq, vd), lambda b, h, i, j: (b * nq + i, h)),
        out_shape=jax.ShapeDtypeStruct((n, n_heads * vd), jnp.bfloat16),
        scratch_shapes=[pltpu.VMEM((2, 1, tq), jnp.float32),
                        pltpu.VMEM((2, 1, tq), jnp.float32),
                        pltpu.VMEM((2, vd, tq), jnp.float32),
                        pltpu.VMEM((3, tk, tq), jnp.float32),
                        pltpu.VMEM((tkb, tq), jnp.float32),
                        pltpu.VMEM((tkb, tq), jnp.float32),
                        pltpu.VMEM((tkb, tq), jnp.bfloat16),
                        pltpu.VMEM((tkb, tq), jnp.bfloat16)],
        compiler_params=_params("parallel", "parallel", "arbitrary", "arbitrary"),
        name="diff_attn",
    )(par, jnp.zeros((1,), jnp.int32), qk, qk, vt, sub_norm.reshape(1, vd))


def differential_attention(x2, g_mix, w_qkv, q_norm, k_norm, lam_p, sub_norm, w_o,
                           lambda_init, *, batch, seq):
    n, dm = x2.shape
    d = DA_HEAD_DIM
    n_heads = w_qkv.shape[1] // (3 * 2 * d)
    h = rmsnorm_bf16(x2, g_mix)
    qk_cols = 2 * n_heads * 2 * d
    log2e = math.log2(math.e)
    gains = jnp.concatenate([jnp.tile(q_norm * (d ** -0.5 * log2e), 2 * n_heads),
                             jnp.tile(k_norm, 2 * n_heads)])
    w_bf = w_qkv.astype(jnp.bfloat16)
    qk = matmul(h, w_bf, n_cols=qk_cols, epilogue="headnorm", extra=gains,
                n_norm_cols=qk_cols, out_dtype=jnp.bfloat16, name="attn_qk")
    vt = matmul(w_bf[:, qk_cols:].T, h, nt=True, out_dtype=jnp.bfloat16, name="attn_vt")
    lp = lam_p.astype(jnp.float32)
    lam = jnp.exp(jnp.sum(lp[0] * lp[1])) - jnp.exp(jnp.sum(lp[2] * lp[3])) + lambda_init
    slopes = 2.0 ** (-8.0 * jnp.arange(1, n_heads + 1, dtype=jnp.float32) / n_heads) * log2e
    par = jnp.concatenate([lam.reshape(1), slopes]).astype(jnp.float32)
    o = diff_attention_core(qk, vt, par, sub_norm, batch=batch, seq=seq, n_heads=n_heads,
                            out_scale=1.0 - lambda_init)
    return matmul(o, w_o.astype(jnp.bfloat16), epilogue="residual", extra=x2, name="attn_out")


def _top_values(s, k):
    rows = lax.broadcasted_iota(jnp.int32, s.shape, 0)
    n_rows = s.shape[0]
    vals = []
    for _ in range(k):
        m = jnp.max(s, axis=0, keepdims=True)
        vals.append(m)
        first = jnp.min(jnp.where(s == m, rows, n_rows), axis=0, keepdims=True)
        s = jnp.where(rows == first, _NEG_INF, s)
    return vals


def _peer_route_kernel(q_ref, keys_ref, s1_ref, e1_ref, s2_ref, e2_ref, tau_ref):
    kk = PEER_TOPK
    half = PEER_HALF
    taus = []
    for p in range(PEER_HEADS):
        qa = q_ref[:, (2 * p) * half:(2 * p + 1) * half]
        qb = q_ref[:, (2 * p + 1) * half:(2 * p + 2) * half]
        s1 = lax.dot_general(keys_ref[p, 0], qa, _NT, preferred_element_type=jnp.float32,
                             precision=lax.Precision.HIGHEST)
        s2 = lax.dot_general(keys_ref[p, 1], qb, _NT, preferred_element_type=jnp.float32,
                             precision=lax.Precision.HIGHEST)
        a = _top_values(s1, kk)
        b = _top_values(s2, kk)
        cands = [a[i] + b[j] for i in range(kk) for j in range(kk) if (i + 1) * (j + 1) <= kk]
        pad = (-len(cands)) % SUBLANES
        cands += [jnp.full_like(a[0], _NEG_INF)] * pad
        top = _top_values(jnp.concatenate(cands, axis=0), kk)
        mx = top[0]
        z = jnp.zeros_like(mx)
        for t in top:
            z = z + jnp.exp(t - mx)
        taus.append(top[kk - 1])
        s1_ref[p] = s1
        s2_ref[p] = s2
        e1_ref[p] = jnp.exp(s1 - a[0])
        e2_ref[p] = jnp.exp(s2 - b[0]) * (0.5 / z)
    tau_ref[...] = jnp.concatenate(taus, axis=0)


def peer_route(q, sub_keys, *, tn=256):
    n = q.shape[0]
    tn = _pick(n, tn)
    ph, nk = PEER_HEADS, PEER_NKEYS
    big = jax.ShapeDtypeStruct((ph, nk, n), jnp.float32)
    bspec = pl.BlockSpec((ph, nk, tn), lambda i: (0, 0, i))
    return pl.pallas_call(
        _peer_route_kernel,
        grid=(n // tn,),
        in_specs=[pl.BlockSpec((tn, q.shape[1]), lambda i: (i, 0)),
                  pl.BlockSpec(sub_keys.shape, lambda i: (0, 0, 0, 0))],
        out_specs=[bspec, bspec, bspec, bspec, pl.BlockSpec((ph, tn), lambda i: (0, i))],
        out_shape=[big, big, big, big, jax.ShapeDtypeStruct((ph, n), jnp.float32)],
        compiler_params=_params("parallel"),
        name="peer_route",
    )(q, sub_keys)


def _peer_dense_kernel(h_ref, u_ref, v_ref, x_ref, s1_ref, e1_ref, s2_ref, e2_ref, tau_ref,
                       o_ref, *, a_per_tile, a_rows):
    e = pl.program_id(1)

    @pl.when(e == 0)
    def _():
        o_ref[...] = x_ref[...]

    act = lax.dot_general(u_ref[...], h_ref[...], _NT, preferred_element_type=jnp.float32)
    act = act * (1.0 + lax.erf(act * (2.0 ** -0.5)))
    nk = PEER_NKEYS
    a_base = lax.rem(e * a_per_tile, a_rows)
    parts = []
    for ai in range(a_per_tile):
        a = a_base + ai
        w = None
        for p in range(PEER_HEADS):
            s1row = s1_ref[p, pl.ds(a, 1), :]
            e1row = e1_ref[p, pl.ds(a, 1), :]
            taurow = tau_ref[pl.ds(p, 1), :]
            sel = jnp.where(s1row + s2_ref[p] >= taurow, e2_ref[p], 0.0) * e1row
            w = sel if w is None else w + sel
        parts.append(act[ai * nk:(ai + 1) * nk, :] * w)
    pt = jnp.concatenate(parts, axis=0)
    o_ref[...] += jnp.dot(pt.T.astype(jnp.bfloat16), v_ref[...], preferred_element_type=jnp.float32)


def peer_dense(hf, u, v, layer, x2, s1, e1, s2, e2, tau, *, tn=512, te=512):
    n, d = hf.shape
    n_exp = u.shape[1]
    tn = _pick(n, tn)
    te = _pick(n_exp, te)
    nk, ph = PEER_NKEYS, PEER_HEADS
    a_per_tile = te // nk
    a_rows = max(a_per_tile, SUBLANES)
    assert a_rows % a_per_tile == 0
    kern = functools.partial(_peer_dense_kernel, a_per_tile=a_per_tile, a_rows=a_rows)
    tok = lambda i, e: (i, 0)
    once = pl.Buffered(1)
    first = pl.BlockSpec((ph, a_rows, tn), lambda i, e: (0, (e * a_per_tile) // a_rows, i))
    second = pl.BlockSpec((ph, nk, tn), lambda i, e: (0, 0, i), pipeline_mode=once)
    return pl.pallas_call(
        kern,
        grid=(n // tn, n_exp // te),
        in_specs=[pl.BlockSpec((tn, d), tok, pipeline_mode=once),
                  pl.BlockSpec((None, te, d), lambda i, e: (layer, e, 0)),
                  pl.BlockSpec((None, te, d), lambda i, e: (layer, e, 0)),
                  pl.BlockSpec((tn, d), tok, pipeline_mode=once),
                  first, first, second, second,
                  pl.BlockSpec((ph, tn), lambda i, e: (0, i))],
        out_specs=pl.BlockSpec((tn, d), tok, pipeline_mode=once),
        out_shape=jax.ShapeDtypeStruct((n, d), jnp.float32),
        compiler_params=_params("parallel", "arbitrary"),
        name="peer_dense",
    )(hf, u, v, x2, s1, e1, s2, e2, tau)


def peer_layer(x2, g_ffn, w_q, sub_keys, u_all, v_all, layer):
    hf = rmsnorm_bf16(x2, g_ffn)
    q = matmul(hf, w_q.astype(jnp.bfloat16), name="peer_q")
    s1, e1, s2, e2, tau = peer_route(q, sub_keys)
    return peer_dense(hf, u_all, v_all, layer, x2, s1, e1, s2, e2, tau)


def _rwkv_mix_kernel(x_ref, xp_ref, xn_ref, g_ref, mu_ref, *o_refs, tm, seq):
    i = pl.program_id(0)
    g = g_ref[...]

    def norm(x):
        ms = jnp.mean(x * x, axis=-1, keepdims=True)
        return x * lax.rsqrt(ms + NORM_EPS) * g

    h = norm(x_ref[...])
    hp = norm(xp_ref[...])[SUBLANES - 1:SUBLANES, :]
    hn = norm(xn_ref[...])[0:1, :]
    t0 = lax.rem(i * tm, seq)
    prev_row = jnp.where(t0 == 0, 0.0, hp)
    next_row = jnp.where(t0 + tm == seq, 0.0, hn)
    rows = lax.broadcasted_iota(jnp.int32, h.shape, 0)
    h_prev = jnp.where(rows == 0, prev_row, pltpu.roll(h, 1, axis=0))
    h_next = jnp.where(rows == tm - 1, next_row, pltpu.roll(h, tm - 1, axis=0))
    xx = 0.5 * (h_prev + h_next) - h
    for idx, o_ref in enumerate(o_refs):
        o_ref[...] = (h + xx * mu_ref[idx:idx + 1, :]).astype(o_ref.dtype)


def rwkv_mix(x2, g_mix, mu, *, seq, tm=256):
    n, d = x2.shape
    tm = _pick(seq, tm)
    assert tm % SUBLANES == 0 and seq % tm == 0
    r8 = tm // SUBLANES
    last = n // SUBLANES - 1
    kern = functools.partial(_rwkv_mix_kernel, tm=tm, seq=seq)
    out = jax.ShapeDtypeStruct((n, d), jnp.bfloat16)
    return pl.pallas_call(
        kern,
        grid=(n // tm,),
        in_specs=[pl.BlockSpec((tm, d), lambda i: (i, 0)),
                  pl.BlockSpec((SUBLANES, d), lambda i: (jnp.maximum(i * r8 - 1, 0), 0)),
                  pl.BlockSpec((SUBLANES, d), lambda i: (jnp.minimum((i + 1) * r8, last), 0)),
                  pl.BlockSpec((1, d), lambda i: (0, 0)),
                  pl.BlockSpec((SUBLANES, d), lambda i: (0, 0))],
        out_specs=[pl.BlockSpec((tm, d), lambda i: (i, 0))] * 6,
        out_shape=[out] * 6,
        compiler_params=_params("parallel"),
        name="rwkv_mix",
    )(x2, x2, x2, g_mix.reshape(1, d), jnp.pad(mu, ((0, SUBLANES - mu.shape[0]), (0, 0))))


HEAD_SUM_WIDTH = 256


def _split3(x):
    hi = x.astype(jnp.bfloat16)
    r1 = x - hi.astype(jnp.float32)
    mid = r1.astype(jnp.bfloat16)
    lo = (r1 - mid.astype(jnp.float32)).astype(jnp.bfloat16)
    return hi, mid, lo


def _head_sum(x, scale):
    w = min(HEAD_SUM_WIDTH, x.shape[1])
    r = lax.broadcasted_iota(jnp.int32, (w, w), 0) // RW_HEAD_SIZE
    c = lax.broadcasted_iota(jnp.int32, (w, w), 1) // RW_HEAD_SIZE
    ones = jnp.where(r == c, scale, 0.0).astype(jnp.bfloat16)
    out = []
    for g in range(x.shape[1] // w):
        terms = _split3(x[:, g * w:(g + 1) * w])
        out.append(sum(jnp.dot(t, ones, preferred_element_type=jnp.float32) for t in terms))
    return out[0] if len(out) == 1 else jnp.concatenate(out, axis=1)


def _rwkv_prep_kernel(k_ref, wl_ref, al_ref, gl_ref, w2_ref, a2_ref, g2_ref, w0_ref, a0_ref,
                      kkw_ref, lw_ref, as_ref, kk_ref, g_ref, *, lora):
    for z in range(2):
        sl = slice(z * lora, (z + 1) * lora)
        wz = w0_ref[z:z + 1, :] + jnp.dot(wl_ref[:, sl], w2_ref[z], preferred_element_type=jnp.float32)
        y = -wz
        softplus = jnp.maximum(y, 0.0) + jnp.log1p(jnp.exp(-jnp.abs(y)))
        lw_ref[z] = -jnp.exp(-softplus - 0.5)
        az = a0_ref[z:z + 1, :] + jnp.dot(al_ref[:, sl], a2_ref[z], preferred_element_type=jnp.float32)
        as_ref[z] = jax.nn.sigmoid(az)
    g_ref[...] = jnp.dot(gl_ref[...], g2_ref[...], preferred_element_type=jnp.float32)
    kr = k_ref[...] * kkw_ref[...]
    ss = _head_sum(kr * kr, 1.0)
    kk_ref[...] = kr / jnp.maximum(jnp.sqrt(ss), 1e-12)


def rwkv_prep(k, wl, al, gl, w2, a2, g2, w0, a0, k_k, *, tm=512, tc=512):
    n, d = k.shape
    tm = _pick(n, tm)
    tc = _pick(d, tc)
    lora = w2.shape[1]
    glw = gl.shape[1]
    kern = functools.partial(_rwkv_prep_kernel, lora=lora)
    row = lambda i, j: (i, 0)
    tile = lambda i, j: (i, j)
    two = jax.ShapeDtypeStruct((2, n, d), jnp.float32)
    one = jax.ShapeDtypeStruct((n, d), jnp.float32)
    return pl.pallas_call(
        kern,
        grid=(n // tm, d // tc),
        in_specs=[pl.BlockSpec((tm, tc), tile),
                  pl.BlockSpec((tm, 2 * lora), row),
                  pl.BlockSpec((tm, 2 * lora), row),
                  pl.BlockSpec((tm, glw), row),
                  pl.BlockSpec((2, lora, tc), lambda i, j: (0, 0, j)),
                  pl.BlockSpec((2, lora, tc), lambda i, j: (0, 0, j)),
                  pl.BlockSpec((glw, tc), lambda i, j: (0, j)),
                  pl.BlockSpec((2, tc), lambda i, j: (0, j)),
                  pl.BlockSpec((2, tc), lambda i, j: (0, j)),
                  pl.BlockSpec((1, tc), lambda i, j: (0, j))],
        out_specs=[pl.BlockSpec((2, tm, tc), lambda i, j: (0, i, j)),
                   pl.BlockSpec((2, tm, tc), lambda i, j: (0, i, j)),
                   pl.BlockSpec((tm, tc), tile),
                   pl.BlockSpec((tm, tc), tile)],
        out_shape=[two, two, one, one],
        compiler_params=_params("parallel", "parallel"),
        name="rwkv_prep",
    )(k, wl, al, gl, w2, a2, g2, w0, a0, k_k.reshape(1, d))


def _bdot(a, b):
    return jnp.dot(a.astype(jnp.bfloat16), b.astype(jnp.bfloat16),
                   preferred_element_type=jnp.float32)


def _rwkv_scan_kernel(r_ref, k_ref, v_ref, kk_ref, lw_ref, as_ref, ka_ref, y_ref, s_scr,
                      *, n_chunks, n_heads, n_batch):
    z = pl.program_id(0)
    c = pl.program_id(2)
    L = RW_CHUNK
    N = RW_HEAD_SIZE
    bf = jnp.bfloat16

    @pl.when(c == 0)
    def _():
        s_scr[...] = jnp.zeros(s_scr.shape, jnp.float32)

    sgn = 1 - 2 * z
    row = lax.broadcasted_iota(jnp.int32, (L, 2 * L), 0)
    col = lax.broadcasted_iota(jnp.int32, (L, 2 * L), 1)
    col = jnp.where(col >= L, col - L, col)
    delta = (col - row) * sgn
    strict2 = delta < 0
    incl2 = delta <= 0
    tri = jnp.where(incl2[:, :L], 1.0, 0.0).astype(bf)
    ka = ka_ref[...]
    sls = [slice(j * N, (j + 1) * N) for j in range(n_heads)]
    chains = [(bi, j) for bi in range(n_batch) for j in range(n_heads)]
    ids = range(len(chains))

    def chunk(ci, carry):
        cc = jnp.where(z == 0, ci, n_chunks - 1 - ci)
        rows = pl.ds(pl.multiple_of(cc * L, L), L)
        g_mat, bk, bkp, vj, w_tot = [], [], [], [], []
        for bi in range(n_batch):
            lw = lw_ref[bi, rows, :]
            a_s = as_ref[bi, rows, :]
            kk = kk_ref[bi, rows, :]
            hi, mid, lo = _split3(lw)
            cw = (jnp.dot(tri, hi, preferred_element_type=jnp.float32)
                  + jnp.dot(tri, mid, preferred_element_type=jnp.float32)
                  + jnp.dot(tri, lo, preferred_element_type=jnp.float32))
            tot = jnp.sum(lw, axis=0, keepdims=True)
            w_inv = jnp.exp(-cw)
            w_rem = jnp.exp(tot - cw)
            kd = k_ref[bi, rows, :] * (1.0 + (a_s - 1.0) * ka)
            ba = kk * a_s
            rt = r_ref[bi, rows, :] * jnp.exp(cw)
            at = -kk * jnp.exp(cw - lw)
            bt = ba * w_inv
            kt = kd * w_inv
            bp = ba * w_rem
            kp = kd * w_rem
            v = v_ref[bi, rows, :]
            wt = jnp.exp(tot)
            for sl in sls:
                g_mat.append(jnp.concatenate([at[:, sl], rt[:, sl]], axis=0).astype(bf))
                bk.append(jnp.concatenate([bt[:, sl], kt[:, sl]], axis=0).astype(bf))
                bkp.append(jnp.concatenate([bp[:, sl], kp[:, sl]], axis=0).astype(bf))
                vj.append(v[:, sl].astype(bf))
                w_tot.append(wt[:, sl])
        s_old = [s_scr[bi, j] for bi, j in chains]
        m12 = [lax.dot_general(g_mat[i], jnp.concatenate([bk[i], s_old[i].astype(bf)], axis=0), _NT,
                               preferred_element_type=jnp.float32) for i in ids]
        a_all = [jnp.where(strict2, m12[i][:L, :2 * L], 0.0) for i in ids]
        r_all = [jnp.where(incl2, m12[i][L:, :2 * L], 0.0).astype(bf) for i in ids]
        x0 = [m12[i][:L, 2 * L:] + _bdot(a_all[i][:, L:], vj[i]) for i in ids]
        w = [jnp.concatenate([a_all[i][:, :L], x0[i]], axis=1) for i in ids]
        keep_x = lax.broadcasted_iota(jnp.int32, (L, L + N), 1) >= L
        zpad = jnp.zeros((N, L + N), bf)
        for st in range(L.bit_length() - 1):
            wb = [w[i].astype(bf) for i in ids]
            w = [jnp.dot(wb[i], jnp.concatenate([wb[i], zpad], axis=0),
                         preferred_element_type=jnp.float32) + jnp.where(keep_x, w[i], 0.0)
                 for i in ids]
        uv = [jnp.concatenate([w[i][:, L:].astype(bf), vj[i]], axis=0) for i in ids]
        for i, (bi, j) in enumerate(chains):
            y_ref[bi, rows, sls[j]] = m12[i][L:, 2 * L:] + jnp.dot(r_all[i], uv[i],
                                                                  preferred_element_type=jnp.float32)
        for i, (bi, j) in enumerate(chains):
            s_scr[bi, j] = s_old[i] * w_tot[i] + lax.dot_general(
                uv[i], bkp[i], _TN, preferred_element_type=jnp.float32)
        return carry

    lax.fori_loop(0, n_chunks, chunk, 0)


def rwkv_scan(r, k, v, kk, lw, asig, k_a, *, batch, seq, heads_per_step=16, rows_per_step=256):
    d = r.shape[-1]
    N, L = RW_HEAD_SIZE, RW_CHUNK
    hw = _pick(d, heads_per_step * N)
    tb = _pick(seq, rows_per_step)
    assert tb % L == 0
    nt = seq // tb
    kern = functools.partial(_rwkv_scan_kernel, n_chunks=tb // L, n_heads=hw // N, n_batch=batch)

    def tmap(z, c):
        return c + z * (nt - 1 - 2 * c)

    shared = pl.BlockSpec((batch, tb, hw), lambda z, g, c: (0, tmap(z, c), g))
    per_dir = pl.BlockSpec((None, batch, tb, hw), lambda z, g, c: (z, 0, tmap(z, c), g))
    return pl.pallas_call(
        kern,
        grid=(2, d // hw, nt),
        in_specs=[shared, shared, shared, shared, per_dir, per_dir,
                  pl.BlockSpec((1, hw), lambda z, g, c: (0, g))],
        out_specs=per_dir,
        out_shape=jax.ShapeDtypeStruct((2, batch, seq, d), jnp.float32),
        scratch_shapes=[pltpu.VMEM((batch, hw // N, N, N), jnp.float32)],
        compiler_params=_params("parallel", "parallel", "arbitrary"),
        name="rwkv_scan",
    )(r, k, v, kk, lw, asig, k_a.reshape(1, d))


def _rwkv_post_kernel(y_ref, r_ref, k_ref, v_ref, as_ref, g_ref, lnw_ref, lnb_ref, ka_ref, rk_ref,
                      o_ref):
    y = y_ref[0] + y_ref[1]
    avg = 1.0 / RW_HEAD_SIZE
    mean = _head_sum(y, avg)
    yc = y - mean
    var = _head_sum(yc * yc, avg)
    yn = yc * lax.rsqrt(var + RW_LNX_EPS) * lnw_ref[...] + lnb_ref[...]
    k_sum = k_ref[...] * (2.0 + (as_ref[0] + as_ref[1] - 2.0) * ka_ref[...])
    bonus = _head_sum(r_ref[...] * k_sum * rk_ref[...], 1.0) * v_ref[...]
    o_ref[...] = ((yn + bonus) * g_ref[...]).astype(o_ref.dtype)


def rwkv_post(y, r, k, v, asig, g, ln_w, ln_b, k_a, r_k, *, tm=256, tc=512):
    n, d = r.shape
    tm = _pick(n, tm)
    tc = _pick(d, tc)
    tile = pl.BlockSpec((tm, tc), lambda i, j: (i, j))
    two = pl.BlockSpec((2, tm, tc), lambda i, j: (0, i, j))
    vec = pl.BlockSpec((1, tc), lambda i, j: (0, j))
    return pl.pallas_call(
        _rwkv_post_kernel,
        grid=(n // tm, d // tc),
        in_specs=[two, tile, tile, tile, two, tile, vec, vec, vec, vec],
        out_specs=tile,
        out_shape=jax.ShapeDtypeStruct((n, d), jnp.bfloat16),
        compiler_params=_params("parallel", "parallel"),
        name="rwkv_post",
    )(y, r, k, v, asig, g, ln_w.reshape(1, d), ln_b.reshape(1, d), k_a.reshape(1, d),
      r_k.reshape(1, d))


def rwkv_layer(x2, g_mix, mu, w_rkv, w0, w1, w2, a0, a1, a2, g1, g2, k_k, k_a, r_k, ln_w, ln_b,
               w_o, *, batch, seq):
    n, d = x2.shape
    bf = jnp.bfloat16
    xr, xk, xv, xw, xa, xg = rwkv_mix(x2, g_mix, mu, seq=seq)
    r = matmul(xr, w_rkv[0].astype(bf), name="rwkv_r")
    k = matmul(xk, w_rkv[1].astype(bf), name="rwkv_k")
    v = matmul(xv, w_rkv[2].astype(bf), name="rwkv_v")
    wl = matmul(xw, jnp.concatenate([w1[0], w1[1]], axis=1).astype(bf), epilogue="tanh",
                out_dtype=bf, name="rwkv_w1")
    al = matmul(xa, jnp.concatenate([a1[0], a1[1]], axis=1).astype(bf), out_dtype=bf,
                name="rwkv_a1")
    gpad = (-g1.shape[1]) % LANES
    gl = matmul(xg, jnp.pad(g1, ((0, 0), (0, gpad))).astype(bf), epilogue="sigmoid",
                out_dtype=bf, name="rwkv_g1")
    lw, asig, kk, g = rwkv_prep(k, wl, al, gl, w2.astype(bf), a2.astype(bf),
                                jnp.pad(g2, ((0, gpad), (0, 0))).astype(bf), w0, a0, k_k)
    shp = (batch, seq, d)
    y = rwkv_scan(r.reshape(shp), k.reshape(shp), v.reshape(shp), kk.reshape(shp),
                  lw.reshape((2,) + shp), asig.reshape((2,) + shp), k_a, batch=batch, seq=seq)
    o = rwkv_post(y.reshape(2, n, d), r, k, v, asig, g, ln_w, ln_b, k_a, r_k)
    return matmul(o, w_o.astype(bf), epilogue="residual", extra=x2, name="rwkv_out")


def kernel(x, norm_mix, norm_ffn, attn_w_qkv, attn_q_norm, attn_k_norm, attn_lambda, attn_sub_norm,
           attn_w_o, rwkv_mu, rwkv_w_rkv, rwkv_w0, rwkv_w1, rwkv_w2, rwkv_a0, rwkv_a1, rwkv_a2,
           rwkv_g1, rwkv_g2, rwkv_k_k, rwkv_k_a, rwkv_r_k, rwkv_ln_w, rwkv_ln_b, rwkv_w_o,
           peer_w_q, peer_sub_keys, peer_u, peer_v):
    batch, seq, d = x.shape
    depth = norm_mix.shape[0]
    x2 = x.reshape(batch * seq, d)
    u_all = peer_u.astype(jnp.bfloat16)
    v_all = peer_v.astype(jnp.bfloat16)
    for i in range(depth):
        j = i // N_MIXERS
        if i % N_MIXERS == 0:
            lambda_init = 0.8 - 0.6 * math.exp(-0.3 * i)
            x2 = differential_attention(x2, norm_mix[i], attn_w_qkv[j], attn_q_norm[j],
                                        attn_k_norm[j], attn_lambda[j], attn_sub_norm[j],
                                        attn_w_o[j], lambda_init, batch=batch, seq=seq)
        else:
            x2 = rwkv_layer(x2, norm_mix[i], rwkv_mu[j], rwkv_w_rkv[j], rwkv_w0[j], rwkv_w1[j],
                            rwkv_w2[j], rwkv_a0[j], rwkv_a1[j], rwkv_a2[j], rwkv_g1[j], rwkv_g2[j],
                            rwkv_k_k[j], rwkv_k_a[j], rwkv_r_k[j].reshape(-1), rwkv_ln_w[j],
                            rwkv_ln_b[j], rwkv_w_o[j], batch=batch, seq=seq)
        x2 = peer_layer(x2, norm_ffn[i], peer_w_q[i], peer_sub_keys[i], u_all, v_all, i)
    return x2.reshape(batch, seq, d)
```

```python
import functools
import math

import jax
import jax.numpy as jnp
from jax import lax
from jax.experimental import pallas as pl
from jax.experimental.pallas import tpu as pltpu

NORM_EPS = 1e-6
N_MIXERS = 2

DA_HEAD_DIM = 128
DA_V_DIM = 2 * DA_HEAD_DIM

RW_HEAD_SIZE = 64
RW_LNX_EPS = RW_HEAD_SIZE * 1e-5
RW_CHUNK = 64

PEER_HEADS = 8
PEER_NKEYS = 128
PEER_HALF = 128
PEER_TOPK = 16

V7X_VMEM_LIMIT_BYTES = 56 * 1024 * 1024
LANES = 128
SUBLANES = 8

_NT = (((1,), (1,)), ((), ()))
_TN = (((0,), (0,)), ((), ()))

_NEG_INF = float("-inf")


def _params(*sem):
    return pltpu.CompilerParams(dimension_semantics=sem,
                                vmem_limit_bytes=V7X_VMEM_LIMIT_BYTES)


def _pick(n, pref):
    if n <= pref:
        return n
    b = pref
    while n % b:
        b //= 2
    return b


def _rmsnorm_kernel(x_ref, g_ref, o_ref):
    x = x_ref[...]
    ms = jnp.mean(x * x, axis=-1, keepdims=True)
    o_ref[...] = (x * lax.rsqrt(ms + NORM_EPS) * g_ref[...]).astype(o_ref.dtype)


def rmsnorm_bf16(x, g):
    n, d = x.shape
    tm = _pick(n, 256)
    return pl.pallas_call(
        _rmsnorm_kernel,
        grid=(n // tm,),
        in_specs=[pl.BlockSpec((tm, d), lambda i: (i, 0)),
                  pl.BlockSpec((1, d), lambda i: (0, 0))],
        out_specs=pl.BlockSpec((tm, d), lambda i: (i, 0)),
        out_shape=jax.ShapeDtypeStruct((n, d), jnp.bfloat16),
        compiler_params=_params("parallel"),
        name="rmsnorm",
    )(x, g.reshape(1, d))


def _mm_kernel(*refs, epilogue, n_norm_tiles, nt):
    a_ref, b_ref = refs[0], refs[1]
    o_ref = refs[-1]
    if nt:
        acc = lax.dot_general(a_ref[...], b_ref[...], _NT, preferred_element_type=jnp.float32)
    else:
        acc = jnp.dot(a_ref[...], b_ref[...], preferred_element_type=jnp.float32)
    if epilogue == "plain":
        o_ref[...] = acc.astype(o_ref.dtype)
    elif epilogue == "residual":
        o_ref[...] = refs[2][...] + acc
    elif epilogue == "tanh":
        o_ref[...] = jnp.tanh(acc).astype(o_ref.dtype)
    elif epilogue == "sigmoid":
        o_ref[...] = jax.nn.sigmoid(acc).astype(o_ref.dtype)
    elif epilogue == "headnorm":
        gain_ref = refs[2]
        j = pl.program_id(1)

        @pl.when(j < n_norm_tiles)
        def _():
            for c in range(acc.shape[1] // DA_HEAD_DIM):
                sl = slice(c * DA_HEAD_DIM, (c + 1) * DA_HEAD_DIM)
                blk = acc[:, sl]
                ms = jnp.mean(blk * blk, axis=-1, keepdims=True)
                o_ref[:, sl] = (blk * lax.rsqrt(ms + NORM_EPS) * gain_ref[:, sl]).astype(o_ref.dtype)

        @pl.when(j >= n_norm_tiles)
        def _():
            o_ref[...] = acc.astype(o_ref.dtype)
    else:
        raise ValueError(epilogue)


def matmul(a, b, *, epilogue="plain", out_dtype=jnp.float32, extra=None,
           n_norm_cols=0, nt=False, n_cols=None, tm=1024, tn=512, name="matmul"):
    m, k = a.shape
    n, k2 = (b.shape if nt else b.shape[::-1])
    assert k == k2
    n = n if n_cols is None else n_cols
    tm = _pick(m, tm)
    tn = _pick(n, tn)
    in_specs = [pl.BlockSpec((tm, k), lambda i, j: (i, 0)),
                pl.BlockSpec((tn, k), lambda i, j: (j, 0)) if nt
                else pl.BlockSpec((k, tn), lambda i, j: (0, j))]
    args = [a, b]
    if epilogue == "residual":
        in_specs.append(pl.BlockSpec((tm, tn), lambda i, j: (i, j)))
        args.append(extra)
    elif epilogue == "headnorm":
        in_specs.append(pl.BlockSpec((1, tn), lambda i, j: (0, j)))
        args.append(extra.reshape(1, n))
        assert n_norm_cols % tn == 0
    kern = functools.partial(_mm_kernel, epilogue=epilogue,
                             n_norm_tiles=n_norm_cols // tn, nt=nt)
    return pl.pallas_call(
        kern,
        grid=(m // tm, n // tn),
        in_specs=in_specs,
        out_specs=pl.BlockSpec((tm, tn), lambda i, j: (i, j)),
        out_shape=jax.ShapeDtypeStruct((m, n), out_dtype),
        compiler_params=_params("parallel", "parallel"),
        name=name,
    )(*args)


ATTN_ROWS = 16


def _attn_kernel(par_ref, zero_ref, q_ref, k_ref, vt_ref, sub_ref, o_ref,
                 m_scr, l_scr, acc_scr, bias_scr, s0_scr, s1_scr, p0_scr, p1_scr,
                 *, tq, tk, tkb, out_scale):
    h = pl.program_id(1)
    qi = pl.program_id(2)
    kj = pl.program_id(3)
    d = DA_HEAD_DIM
    R = ATTN_ROWS
    n_chunk = tk // R
    n_kb = tkb // tk
    slope = par_ref[1 + h]

    @pl.when(kj == 0)
    def _():
        m_scr[...] = jnp.full(m_scr.shape, _NEG_INF, jnp.float32)
        l_scr[...] = jnp.zeros(l_scr.shape, jnp.float32)
        acc_scr[...] = jnp.zeros(acc_scr.shape, jnp.float32)

    @pl.when((kj == 0) & (qi == 0))
    def _():
        for r in range(n_chunk):
            rows = slice(r * R, (r + 1) * R)
            rel = (lax.broadcasted_iota(jnp.int32, (R, tq), 0) + (r * R)
                   - lax.broadcasted_iota(jnp.int32, (R, tq), 1)).astype(jnp.float32)
            bias_scr[0, rows, :] = rel * (-slope)
            bias_scr[1, rows, :] = rel * slope
            bias_scr[2, rows, :] = jnp.abs(rel) * (-slope)

    tiles, shifts = [], []
    for kb in range(n_kb):
        off = kj * tkb + kb * tk - qi * tq
        tiles.append(jnp.where(off > 0, 0, jnp.where(off < 0, 1, 2)))
        shifts.append(jnp.abs(off).astype(jnp.float32) * (-slope))

    z0 = zero_ref[0]
    s_scr = (s0_scr, s1_scr)
    p_scr = (p0_scr, p1_scr)

    def rows_of(ref, start):
        return ref[pl.ds(pl.multiple_of(z0 + start, R), R), :]

    q = q_ref[...]

    def scores(c, kb):
        sv = lax.dot_general(k_ref[kb * tk:(kb + 1) * tk, c * d:(c + 1) * d], q[:, c * d:(c + 1) * d],
                             _NT, preferred_element_type=jnp.float32)
        part = jnp.full((SUBLANES, tq), _NEG_INF, jnp.float32)
        for r in range(n_chunk):
            s = sv[r * R:(r + 1) * R, :] + bias_scr[tiles[kb], r * R:(r + 1) * R, :]
            s_scr[c][kb * tk + r * R:kb * tk + (r + 1) * R, :] = s
            part = jnp.maximum(part, jnp.max(s.reshape(R // SUBLANES, SUBLANES, tq), axis=0))
        return part + shifts[kb]

    blk_maxes = []
    for c in range(2):
        blk_max = scores(c, 0)
        for kb in range(1, n_kb):
            blk_max = jnp.maximum(blk_max, scores(c, kb))
        blk_maxes.append(blk_max)
    for c in range(2):
        m_prev = m_scr[c]
        m_new = jnp.maximum(m_prev, jnp.max(blk_maxes[c], axis=0, keepdims=True))
        alpha = jnp.exp2(m_prev - m_new)
        lsum = jnp.zeros((SUBLANES, tq), jnp.float32)
        for kb in range(n_kb):
            m_kb = m_new - shifts[kb]
            for r in range(n_chunk):
                p = jnp.exp2(rows_of(s_scr[c], kb * tk + r * R) - m_kb)
                p_scr[c][kb * tk + r * R:kb * tk + (r + 1) * R, :] = p.astype(jnp.bfloat16)
                lsum = lsum + jnp.sum(p.reshape(R // SUBLANES, SUBLANES, tq), axis=0)
        l_scr[c] = alpha * l_scr[c] + jnp.sum(lsum, axis=0, keepdims=True)
        acc_scr[c] = acc_scr[c] * alpha + jnp.dot(vt_ref[...], p_scr[c][...],
                                                  preferred_element_type=jnp.float32)
        m_scr[c] = m_new

    @pl.when(kj == pl.num_programs(3) - 1)
    def _():
        lam = par_ref[0]
        ot = acc_scr[0] * (1.0 / l_scr[0]) - acc_scr[1] * (lam / l_scr[1])
        ms = jnp.mean(ot * ot, axis=0, keepdims=True)
        o = (ot * lax.rsqrt(ms + NORM_EPS)).T
        o_ref[...] = (o * (sub_ref[...] * out_scale)).astype(o_ref.dtype)


def diff_attention_core(qk, vt, par, sub_norm, *, batch, seq, n_heads, out_scale,
                        tq=512, tk=512, tkb=2048):
    n = batch * seq
    tq = _pick(seq, tq)
    tkb = _pick(seq, tkb)
    tk = tq
    assert tkb % tk == 0 and tk % ATTN_ROWS == 0
    nq, nk = seq // tq, seq // tkb
    vd = DA_V_DIM
    kern = functools.partial(_attn_kernel, tq=tq, tk=tk, tkb=tkb, out_scale=out_scale)
    return pl.pallas_call(
        kern,
        grid=(batch, n_heads, nq, nk),
        in_specs=[
            pl.BlockSpec(memory_space=pltpu.SMEM),
            pl.BlockSpec(memory_space=pltpu.SMEM),
            pl.BlockSpec((tq, vd), lambda b, h, i, j: (b * nq + i, h)),
            pl.BlockSpec((tkb, vd), lambda b, h, i, j: (b * nk + j, n_heads + h)),
            pl.BlockSpec((vd, tkb), lambda b, h, i, j: (h, b * nk + j)),
            pl.BlockSpec((1, vd), lambda b, h, i, j: (0, 0)),
        ],
        out_specs=pl.BlockSpec((tq, vd), lambda b, h, i, j: (b * nq + i, h)),
        out_shape=jax.ShapeDtypeStruct((n, n_heads * vd), jnp.bfloat16),
        scratch_shapes=[pltpu.VMEM((2, 1, tq), jnp.float32),
                        pltpu.VMEM((2, 1, tq), jnp.float32),
                        pltpu.VMEM((2, vd, tq), jnp.float32),
                        pltpu.VMEM((3, tk, tq), jnp.float32),
                        pltpu.VMEM((tkb, tq), jnp.float32),
                        pltpu.VMEM((tkb, tq), jnp.float32),
                        pltpu.VMEM((tkb, tq), jnp.bfloat16),
                        pltpu.VMEM((tkb, tq), jnp.bfloat16)],
        compiler_params=_params("parallel", "parallel", "arbitrary", "arbitrary"),
        name="diff_attn",
    )(par, jnp.zeros((1,), jnp.int32), qk, qk, vt, sub_norm.reshape(1, vd))


def differential_attention(x2, g_mix, w_qkv, q_norm, k_norm, lam_p, sub_norm, w_o,
                           lambda_init, *, batch, seq):
    n, dm = x2.shape
    d = DA_HEAD_DIM
    n_heads = w_qkv.shape[1] // (3 * 2 * d)
    h = rmsnorm_bf16(x2, g_mix)
    qk_cols = 2 * n_heads * 2 * d
    log2e = math.log2(math.e)
    gains = jnp.concatenate([jnp.tile(q_norm * (d ** -0.5 * log2e), 2 * n_heads),
                             jnp.tile(k_norm, 2 * n_heads)])
    w_bf = w_qkv.astype(jnp.bfloat16)
    qk = matmul(h, w_bf, n_cols=qk_cols, epilogue="headnorm", extra=gains,
                n_norm_cols=qk_cols, out_dtype=jnp.bfloat16, name="attn_qk")
    vt = matmul(w_bf[:, qk_cols:].T, h, nt=True, out_dtype=jnp.bfloat16, name="attn_vt")
    lp = lam_p.astype(jnp.float32)
    lam = jnp.exp(jnp.sum(lp[0] * lp[1])) - jnp.exp(jnp.sum(lp[2] * lp[3])) + lambda_init
    slopes = 2.0 ** (-8.0 * jnp.arange(1, n_heads + 1, dtype=jnp.float32) / n_heads) * log2e
    par = jnp.concatenate([lam.reshape(1), slopes]).astype(jnp.float32)
    o = diff_attention_core(qk, vt, par, sub_norm, batch=batch, seq=seq, n_heads=n_heads,
                            out_scale=1.0 - lambda_init)
    return matmul(o, w_o.astype(jnp.bfloat16), epilogue="residual", extra=x2, name="attn_out")


def _top_values(s, k):
    rows = lax.broadcasted_iota(jnp.int32, s.shape, 0)
    n_rows = s.shape[0]
    vals = []
    for _ in range(k):
        m = jnp.max(s, axis=0, keepdims=True)
        vals.append(m)
        first = jnp.min(jnp.where(s == m, rows, n_rows), axis=0, keepdims=True)
        s = jnp.where(rows == first, _NEG_INF, s)
    return vals


def _compare_exchange(v, i, j):
    hi = jnp.maximum(v[i], v[j])
    lo = jnp.minimum(v[i], v[j])
    v[i], v[j] = hi, lo


def _bitonic_merge_desc(v):
    n = len(v)
    j = n // 2
    while j >= 1:
        for i in range(n):
            if i & j == 0:
                _compare_exchange(v, i, i | j)
        j //= 2


def _sort_desc(v):
    n = len(v)
    k = 2
    while k <= n:
        j = k // 2
        while j >= 1:
            for i in range(n):
                p = i ^ j
                if p > i:
                    if i & k == 0:
                        _compare_exchange(v, i, p)
                    else:
                        _compare_exchange(v, p, i)
            j //= 2
        k *= 2


def _top_values_net(s, k):
    assert s.shape[0] == k * SUBLANES
    v = [s[SUBLANES * i:SUBLANES * (i + 1), :] for i in range(k)]
    _sort_desc(v)
    shift = SUBLANES // 2
    while shift >= 1:
        w = [pltpu.roll(x, shift, axis=0) for x in v]
        v = [jnp.maximum(v[i], w[k - 1 - i]) for i in range(k)]
        _bitonic_merge_desc(v)
        shift //= 2
    return [x[0:1, :] for x in v]


def _peer_route_kernel(q_ref, keys_ref, s1_ref, e1_ref, s2_ref, e2_ref, tau_ref):
    kk = PEER_TOPK
    half = PEER_HALF
    taus = []
    for p in range(PEER_HEADS):
        qa = q_ref[:, (2 * p) * half:(2 * p + 1) * half]
        qb = q_ref[:, (2 * p + 1) * half:(2 * p + 2) * half]
        s1 = lax.dot_general(keys_ref[p, 0], qa, _NT, preferred_element_type=jnp.float32,
                             precision=lax.Precision.HIGHEST)
        s2 = lax.dot_general(keys_ref[p, 1], qb, _NT, preferred_element_type=jnp.float32,
                             precision=lax.Precision.HIGHEST)
        a = _top_values_net(s1, kk)
        b = _top_values_net(s2, kk)
        cands = [a[i] + b[j] for i in range(kk) for j in range(kk) if (i + 1) * (j + 1) <= kk]
        pad = (-len(cands)) % SUBLANES
        cands += [jnp.full_like(a[0], _NEG_INF)] * pad
        top = _top_values(jnp.concatenate(cands, axis=0), kk)
        mx = top[0]
        z = jnp.zeros_like(mx)
        for t in top:
            z = z + jnp.exp(t - mx)
        taus.append(top[kk - 1])
        s1_ref[p] = s1
        s2_ref[p] = s2
        e1_ref[p] = jnp.exp(s1 - a[0])
        e2_ref[p] = jnp.exp(s2 - b[0]) * (0.5 / z)
    tau_ref[...] = jnp.concatenate(taus, axis=0)


def peer_route(q, sub_keys, *, tn=256):
    n = q.shape[0]
    tn = _pick(n, tn)
    ph, nk = PEER_HEADS, PEER_NKEYS
    big = jax.ShapeDtypeStruct((ph, nk, n), jnp.float32)
    bspec = pl.BlockSpec((ph, nk, tn), lambda i: (0, 0, i))
    return pl.pallas_call(
        _peer_route_kernel,
        grid=(n // tn,),
        in_specs=[pl.BlockSpec((tn, q.shape[1]), lambda i: (i, 0)),
                  pl.BlockSpec(sub_keys.shape, lambda i: (0, 0, 0, 0))],
        out_specs=[bspec, bspec, bspec, bspec, pl.BlockSpec((ph, tn), lambda i: (0, i))],
        out_shape=[big, big, big, big, jax.ShapeDtypeStruct((ph, n), jnp.float32)],
        compiler_params=_params("parallel"),
        name="peer_route",
    )(q, sub_keys)


def _peer_dense_kernel(h_ref, u_ref, v_ref, x_ref, s1_ref, e1_ref, s2_ref, e2_ref, tau_ref,
                       o_ref, *, a_per_tile, a_rows):
    e = pl.program_id(1)

    @pl.when(e == 0)
    def _():
        o_ref[...] = x_ref[...]

    act = lax.dot_general(u_ref[...], h_ref[...], _NT, preferred_element_type=jnp.float32)
    act = act * (1.0 + lax.erf(act * (2.0 ** -0.5)))
    nk = PEER_NKEYS
    a_base = lax.rem(e * a_per_tile, a_rows)
    parts = []
    for ai in range(a_per_tile):
        a = a_base + ai
        w = None
        for p in range(PEER_HEADS):
            s1row = s1_ref[p, pl.ds(a, 1), :]
            e1row = e1_ref[p, pl.ds(a, 1), :]
            taurow = tau_ref[pl.ds(p, 1), :]
            sel = jnp.where(s1row + s2_ref[p] >= taurow, e2_ref[p], 0.0) * e1row
            w = sel if w is None else w + sel
        parts.append(act[ai * nk:(ai + 1) * nk, :] * w)
    pt = jnp.concatenate(parts, axis=0)
    o_ref[...] += jnp.dot(pt.T.astype(jnp.bfloat16), v_ref[...], preferred_element_type=jnp.float32)


def peer_dense(hf, u, v, layer, x2, s1, e1, s2, e2, tau, *, tn=512, te=512):
    n, d = hf.shape
    n_exp = u.shape[1]
    tn = _pick(n, tn)
    te = _pick(n_exp, te)
    nk, ph = PEER_NKEYS, PEER_HEADS
    a_per_tile = te // nk
    a_rows = max(a_per_tile, SUBLANES)
    assert a_rows % a_per_tile == 0
    kern = functools.partial(_peer_dense_kernel, a_per_tile=a_per_tile, a_rows=a_rows)
    tok = lambda i, e: (i, 0)
    once = pl.Buffered(1)
    first = pl.BlockSpec((ph, a_rows, tn), lambda i, e: (0, (e * a_per_tile) // a_rows, i))
    second = pl.BlockSpec((ph, nk, tn), lambda i, e: (0, 0, i), pipeline_mode=once)
    return pl.pallas_call(
        kern,
        grid=(n // tn, n_exp // te),
        in_specs=[pl.BlockSpec((tn, d), tok, pipeline_mode=once),
                  pl.BlockSpec((None, te, d), lambda i, e: (layer, e, 0)),
                  pl.BlockSpec((None, te, d), lambda i, e: (layer, e, 0)),
                  pl.BlockSpec((tn, d), tok, pipeline_mode=once),
                  first, first, second, second,
                  pl.BlockSpec((ph, tn), lambda i, e: (0, i))],
        out_specs=pl.BlockSpec((tn, d), tok, pipeline_mode=once),
        out_shape=jax.ShapeDtypeStruct((n, d), jnp.float32),
        compiler_params=_params("parallel", "arbitrary"),
        name="peer_dense",
    )(hf, u, v, x2, s1, e1, s2, e2, tau)


def peer_layer(x2, g_ffn, w_q, sub_keys, u_all, v_all, layer):
    hf = rmsnorm_bf16(x2, g_ffn)
    q = matmul(hf, w_q.astype(jnp.bfloat16), name="peer_q")
    s1, e1, s2, e2, tau = peer_route(q, sub_keys)
    return peer_dense(hf, u_all, v_all, layer, x2, s1, e1, s2, e2, tau)


def _rwkv_mix_kernel(x_ref, xp_ref, xn_ref, g_ref, mu_ref, *o_refs, tm, seq):
    i = pl.program_id(0)
    g = g_ref[...]

    def norm(x):
        ms = jnp.mean(x * x, axis=-1, keepdims=True)
        return x * lax.rsqrt(ms + NORM_EPS) * g

    h = norm(x_ref[...])
    hp = norm(xp_ref[...])[SUBLANES - 1:SUBLANES, :]
    hn = norm(xn_ref[...])[0:1, :]
    t0 = lax.rem(i * tm, seq)
    prev_row = jnp.where(t0 == 0, 0.0, hp)
    next_row = jnp.where(t0 + tm == seq, 0.0, hn)
    rows = lax.broadcasted_iota(jnp.int32, h.shape, 0)
    h_prev = jnp.where(rows == 0, prev_row, pltpu.roll(h, 1, axis=0))
    h_next = jnp.where(rows == tm - 1, next_row, pltpu.roll(h, tm - 1, axis=0))
    xx = 0.5 * (h_prev + h_next) - h
    for idx, o_ref in enumerate(o_refs):
        o_ref[...] = (h + xx * mu_ref[idx:idx + 1, :]).astype(o_ref.dtype)


def rwkv_mix(x2, g_mix, mu, *, seq, tm=256):
    n, d = x2.shape
    tm = _pick(seq, tm)
    assert tm % SUBLANES == 0 and seq % tm == 0
    r8 = tm // SUBLANES
    last = n // SUBLANES - 1
    kern = functools.partial(_rwkv_mix_kernel, tm=tm, seq=seq)
    out = jax.ShapeDtypeStruct((n, d), jnp.bfloat16)
    return pl.pallas_call(
        kern,
        grid=(n // tm,),
        in_specs=[pl.BlockSpec((tm, d), lambda i: (i, 0)),
                  pl.BlockSpec((SUBLANES, d), lambda i: (jnp.maximum(i * r8 - 1, 0), 0)),
                  pl.BlockSpec((SUBLANES, d), lambda i: (jnp.minimum((i + 1) * r8, last), 0)),
                  pl.BlockSpec((1, d), lambda i: (0, 0)),
                  pl.BlockSpec((SUBLANES, d), lambda i: (0, 0))],
        out_specs=[pl.BlockSpec((tm, d), lambda i: (i, 0))] * 6,
        out_shape=[out] * 6,
        compiler_params=_params("parallel"),
        name="rwkv_mix",
    )(x2, x2, x2, g_mix.reshape(1, d), jnp.pad(mu, ((0, SUBLANES - mu.shape[0]), (0, 0))))


HEAD_SUM_WIDTH = 256


def _split3(x):
    hi = x.astype(jnp.bfloat16)
    r1 = x - hi.astype(jnp.float32)
    mid = r1.astype(jnp.bfloat16)
    lo = (r1 - mid.astype(jnp.float32)).astype(jnp.bfloat16)
    return hi, mid, lo


def _head_sum(x, scale):
    w = min(HEAD_SUM_WIDTH, x.shape[1])
    r = lax.broadcasted_iota(jnp.int32, (w, w), 0) // RW_HEAD_SIZE
    c = lax.broadcasted_iota(jnp.int32, (w, w), 1) // RW_HEAD_SIZE
    ones = jnp.where(r == c, scale, 0.0).astype(jnp.bfloat16)
    out = []
    for g in range(x.shape[1] // w):
        terms = _split3(x[:, g * w:(g + 1) * w])
        out.append(sum(jnp.dot(t, ones, preferred_element_type=jnp.float32) for t in terms))
    return out[0] if len(out) == 1 else jnp.concatenate(out, axis=1)


def _rwkv_prep_kernel(k_ref, wl_ref, al_ref, gl_ref, w2_ref, a2_ref, g2_ref, w0_ref, a0_ref,
                      kkw_ref, lw_ref, as_ref, kk_ref, g_ref, *, lora):
    for z in range(2):
        sl = slice(z * lora, (z + 1) * lora)
        wz = w0_ref[z:z + 1, :] + jnp.dot(wl_ref[:, sl], w2_ref[z], preferred_element_type=jnp.float32)
        y = -wz
        softplus = jnp.maximum(y, 0.0) + jnp.log1p(jnp.exp(-jnp.abs(y)))
        lw_ref[z] = -jnp.exp(-softplus - 0.5)
        az = a0_ref[z:z + 1, :] + jnp.dot(al_ref[:, sl], a2_ref[z], preferred_element_type=jnp.float32)
        as_ref[z] = jax.nn.sigmoid(az)
    g_ref[...] = jnp.dot(gl_ref[...], g2_ref[...], preferred_element_type=jnp.float32)
    kr = k_ref[...] * kkw_ref[...]
    ss = _head_sum(kr * kr, 1.0)
    kk_ref[...] = kr / jnp.maximum(jnp.sqrt(ss), 1e-12)


def rwkv_prep(k, wl, al, gl, w2, a2, g2, w0, a0, k_k, *, tm=512, tc=512):
    n, d = k.shape
    tm = _pick(n, tm)
    tc = _pick(d, tc)
    lora = w2.shape[1]
    glw = gl.shape[1]
    kern = functools.partial(_rwkv_prep_kernel, lora=lora)
    row = lambda i, j: (i, 0)
    tile = lambda i, j: (i, j)
    two = jax.ShapeDtypeStruct((2, n, d), jnp.float32)
    one = jax.ShapeDtypeStruct((n, d), jnp.float32)
    return pl.pallas_call(
        kern,
        grid=(n // tm, d // tc),
        in_specs=[pl.BlockSpec((tm, tc), tile),
                  pl.BlockSpec((tm, 2 * lora), row),
                  pl.BlockSpec((tm, 2 * lora), row),
                  pl.BlockSpec((tm, glw), row),
                  pl.BlockSpec((2, lora, tc), lambda i, j: (0, 0, j)),
                  pl.BlockSpec((2, lora, tc), lambda i, j: (0, 0, j)),
                  pl.BlockSpec((glw, tc), lambda i, j: (0, j)),
                  pl.BlockSpec((2, tc), lambda i, j: (0, j)),
                  pl.BlockSpec((2, tc), lambda i, j: (0, j)),
                  pl.BlockSpec((1, tc), lambda i, j: (0, j))],
        out_specs=[pl.BlockSpec((2, tm, tc), lambda i, j: (0, i, j)),
                   pl.BlockSpec((2, tm, tc), lambda i, j: (0, i, j)),
                   pl.BlockSpec((tm, tc), tile),
                   pl.BlockSpec((tm, tc), tile)],
        out_shape=[two, two, one, one],
        compiler_params=_params("parallel", "parallel"),
        name="rwkv_prep",
    )(k, wl, al, gl, w2, a2, g2, w0, a0, k_k.reshape(1, d))


def _bdot(a, b):
    return jnp.dot(a.astype(jnp.bfloat16), b.astype(jnp.bfloat16),
                   preferred_element_type=jnp.float32)


def _rwkv_scan_kernel(r_ref, k_ref, v_ref, kk_ref, lw_ref, as_ref, ka_ref, y_ref, s_scr,
                      *, n_chunks, n_heads, n_batch):
    z = pl.program_id(0)
    c = pl.program_id(2)
    L = RW_CHUNK
    N = RW_HEAD_SIZE
    bf = jnp.bfloat16

    @pl.when(c == 0)
    def _():
        s_scr[...] = jnp.zeros(s_scr.shape, jnp.float32)

    sgn = 1 - 2 * z
    row = lax.broadcasted_iota(jnp.int32, (L, 2 * L), 0)
    col = lax.broadcasted_iota(jnp.int32, (L, 2 * L), 1)
    col = jnp.where(col >= L, col - L, col)
    delta = (col - row) * sgn
    strict2 = delta < 0
    incl2 = delta <= 0
    tri = jnp.where(incl2[:, :L], 1.0, 0.0).astype(bf)
    ka = ka_ref[...]
    sls = [slice(j * N, (j + 1) * N) for j in range(n_heads)]
    chains = [(bi, j) for bi in range(n_batch) for j in range(n_heads)]
    ids = range(len(chains))

    def chunk(ci, carry):
        cc = jnp.where(z == 0, ci, n_chunks - 1 - ci)
        rows = pl.ds(pl.multiple_of(cc * L, L), L)
        g_mat, bk, bkp, vj, w_tot = [], [], [], [], []
        for bi in range(n_batch):
            lw = lw_ref[bi, rows, :]
            a_s = as_ref[bi, rows, :]
            kk = kk_ref[bi, rows, :]
            hi, mid, lo = _split3(lw)
            cw = (jnp.dot(tri, hi, preferred_element_type=jnp.float32)
                  + jnp.dot(tri, mid, preferred_element_type=jnp.float32)
                  + jnp.dot(tri, lo, preferred_element_type=jnp.float32))
            tot = jnp.sum(lw, axis=0, keepdims=True)
            w_inv = jnp.exp(-cw)
            w_rem = jnp.exp(tot - cw)
            kd = k_ref[bi, rows, :] * (1.0 + (a_s - 1.0) * ka)
            ba = kk * a_s
            rt = r_ref[bi, rows, :] * jnp.exp(cw)
            at = -kk * jnp.exp(cw - lw)
            bt = ba * w_inv
            kt = kd * w_inv
            bp = ba * w_rem
            kp = kd * w_rem
            v = v_ref[bi, rows, :]
            wt = jnp.exp(tot)
            for sl in sls:
                g_mat.append(jnp.concatenate([at[:, sl], rt[:, sl]], axis=0).astype(bf))
                bk.append(jnp.concatenate([bt[:, sl], kt[:, sl]], axis=0).astype(bf))
                bkp.append(jnp.concatenate([bp[:, sl], kp[:, sl]], axis=0).astype(bf))
                vj.append(v[:, sl].astype(bf))
                w_tot.append(wt[:, sl])
        s_old = [s_scr[bi, j] for bi, j in chains]
        m12 = [lax.dot_general(g_mat[i], jnp.concatenate([bk[i], s_old[i].astype(bf)], axis=0), _NT,
                               preferred_element_type=jnp.float32) for i in ids]
        a_all = [jnp.where(strict2, m12[i][:L, :2 * L], 0.0) for i in ids]
        r_all = [jnp.where(incl2, m12[i][L:, :2 * L], 0.0).astype(bf) for i in ids]
        x0 = [m12[i][:L, 2 * L:] + _bdot(a_all[i][:, L:], vj[i]) for i in ids]
        w = [jnp.concatenate([a_all[i][:, :L], x0[i]], axis=1) for i in ids]
        keep_x = lax.broadcasted_iota(jnp.int32, (L, L + N), 1) >= L
        zpad = jnp.zeros((N, L + N), bf)
        for st in range(L.bit_length() - 1):
            wb = [w[i].astype(bf) for i in ids]
            w = [jnp.dot(wb[i], jnp.concatenate([wb[i], zpad], axis=0),
                         preferred_element_type=jnp.float32) + jnp.where(keep_x, w[i], 0.0)
                 for i in ids]
        uv = [jnp.concatenate([w[i][:, L:].astype(bf), vj[i]], axis=0) for i in ids]
        for i, (bi, j) in enumerate(chains):
            y_ref[bi, rows, sls[j]] = m12[i][L:, 2 * L:] + jnp.dot(r_all[i], uv[i],
                                                                  preferred_element_type=jnp.float32)
        for i, (bi, j) in enumerate(chains):
            s_scr[bi, j] = s_old[i] * w_tot[i] + lax.dot_general(
                uv[i], bkp[i], _TN, preferred_element_type=jnp.float32)
        return carry

    lax.fori_loop(0, n_chunks, chunk, 0)


def rwkv_scan(r, k, v, kk, lw, asig, k_a, *, batch, seq, heads_per_step=16, rows_per_step=256):
    d = r.shape[-1]
    N, L = RW_HEAD_SIZE, RW_CHUNK
    hw = _pick(d, heads_per_step * N)
    tb = _pick(seq, rows_per_step)
    assert tb % L == 0
    nt = seq // tb
    kern = functools.partial(_rwkv_scan_kernel, n_chunks=tb // L, n_heads=hw // N, n_batch=batch)

    def tmap(z, c):
        return c + z * (nt - 1 - 2 * c)

    shared = pl.BlockSpec((batch, tb, hw), lambda z, g, c: (0, tmap(z, c), g))
    per_dir = pl.BlockSpec((None, batch, tb, hw), lambda z, g, c: (z, 0, tmap(z, c), g))
    return pl.pallas_call(
        kern,
        grid=(2, d // hw, nt),
        in_specs=[shared, shared, shared, shared, per_dir, per_dir,
                  pl.BlockSpec((1, hw), lambda z, g, c: (0, g))],
        out_specs=per_dir,
        out_shape=jax.ShapeDtypeStruct((2, batch, seq, d), jnp.float32),
        scratch_shapes=[pltpu.VMEM((batch, hw // N, N, N), jnp.float32)],
        compiler_params=_params("parallel", "parallel", "arbitrary"),
        name="rwkv_scan",
    )(r, k, v, kk, lw, asig, k_a.reshape(1, d))


def _rwkv_post_kernel(y_ref, r_ref, k_ref, v_ref, as_ref, g_ref, lnw_ref, lnb_ref, ka_ref, rk_ref,
                      o_ref):
    y = y_ref[0] + y_ref[1]
    avg = 1.0 / RW_HEAD_SIZE
    mean = _head_sum(y, avg)
    yc = y - mean
    var = _head_sum(yc * yc, avg)
    yn = yc * lax.rsqrt(var + RW_LNX_EPS) * lnw_ref[...] + lnb_ref[...]
    k_sum = k_ref[...] * (2.0 + (as_ref[0] + as_ref[1] - 2.0) * ka_ref[...])
    bonus = _head_sum(r_ref[...] * k_sum * rk_ref[...], 1.0) * v_ref[...]
    o_ref[...] = ((yn + bonus) * g_ref[...]).astype(o_ref.dtype)


def rwkv_post(y, r, k, v, asig, g, ln_w, ln_b, k_a, r_k, *, tm=256, tc=512):
    n, d = r.shape
    tm = _pick(n, tm)
    tc = _pick(d, tc)
    tile = pl.BlockSpec((tm, tc), lambda i, j: (i, j))
    two = pl.BlockSpec((2, tm, tc), lambda i, j: (0, i, j))
    vec = pl.BlockSpec((1, tc), lambda i, j: (0, j))
    return pl.pallas_call(
        _rwkv_post_kernel,
        grid=(n // tm, d // tc),
        in_specs=[two, tile, tile, tile, two, tile, vec, vec, vec, vec],
        out_specs=tile,
        out_shape=jax.ShapeDtypeStruct((n, d), jnp.bfloat16),
        compiler_params=_params("parallel", "parallel"),
        name="rwkv_post",
    )(y, r, k, v, asig, g, ln_w.reshape(1, d), ln_b.reshape(1, d), k_a.reshape(1, d),
      r_k.reshape(1, d))


def rwkv_layer(x2, g_mix, mu, w_rkv, w0, w1, w2, a0, a1, a2, g1, g2, k_k, k_a, r_k, ln_w, ln_b,
               w_o, *, batch, seq):
    n, d = x2.shape
    bf = jnp.bfloat16
    xr, xk, xv, xw, xa, xg = rwkv_mix(x2, g_mix, mu, seq=seq)
    r = matmul(xr, w_rkv[0].astype(bf), name="rwkv_r")
    k = matmul(xk, w_rkv[1].astype(bf), name="rwkv_k")
    v = matmul(xv, w_rkv[2].astype(bf), name="rwkv_v")
    wl = matmul(xw, jnp.concatenate([w1[0], w1[1]], axis=1).astype(bf), epilogue="tanh",
                out_dtype=bf, name="rwkv_w1")
    al = matmul(xa, jnp.concatenate([a1[0], a1[1]], axis=1).astype(bf), out_dtype=bf,
                name="rwkv_a1")
    gpad = (-g1.shape[1]) % LANES
    gl = matmul(xg, jnp.pad(g1, ((0, 0), (0, gpad))).astype(bf), epilogue="sigmoid",
                out_dtype=bf, name="rwkv_g1")
    lw, asig, kk, g = rwkv_prep(k, wl, al, gl, w2.astype(bf), a2.astype(bf),
                                jnp.pad(g2, ((0, gpad), (0, 0))).astype(bf), w0, a0, k_k)
    shp = (batch, seq, d)
    y = rwkv_scan(r.reshape(shp), k.reshape(shp), v.reshape(shp), kk.reshape(shp),
                  lw.reshape((2,) + shp), asig.reshape((2,) + shp), k_a, batch=batch, seq=seq)
    o = rwkv_post(y.reshape(2, n, d), r, k, v, asig, g, ln_w, ln_b, k_a, r_k)
    return matmul(o, w_o.astype(bf), epilogue="residual", extra=x2, name="rwkv_out")


def kernel(x, norm_mix, norm_ffn, attn_w_qkv, attn_q_norm, attn_k_norm, attn_lambda, attn_sub_norm,
           attn_w_o, rwkv_mu, rwkv_w_rkv, rwkv_w0, rwkv_w1, rwkv_w2, rwkv_a0, rwkv_a1, rwkv_a2,
           rwkv_g1, rwkv_g2, rwkv_k_k, rwkv_k_a, rwkv_r_k, rwkv_ln_w, rwkv_ln_b, rwkv_w_o,
           peer_w_q, peer_sub_keys, peer_u, peer_v):
    batch, seq, d = x.shape
    depth = norm_mix.shape[0]
    x2 = x.reshape(batch * seq, d)
    u_all = peer_u.astype(jnp.bfloat16)
    v_all = peer_v.astype(jnp.bfloat16)
    for i in range(depth):
        j = i // N_MIXERS
        if i % N_MIXERS == 0:
            lambda_init = 0.8 - 0.6 * math.exp(-0.3 * i)
            x2 = differential_attention(x2, norm_mix[i], attn_w_qkv[j], attn_q_norm[j],
                                        attn_k_norm[j], attn_lambda[j], attn_sub_norm[j],
                                        attn_w_o[j], lambda_init, batch=batch, seq=seq)
        else:
            x2 = rwkv_layer(x2, norm_mix[i], rwkv_mu[j], rwkv_w_rkv[j], rwkv_w0[j], rwkv_w1[j],
                            rwkv_w2[j], rwkv_a0[j], rwkv_a1[j], rwkv_a2[j], rwkv_g1[j], rwkv_g2[j],
                            rwkv_k_k[j], rwkv_k_a[j], rwkv_r_k[j].reshape(-1), rwkv_ln_w[j],
                            rwkv_ln_b[j], rwkv_w_o[j], batch=batch, seq=seq)
        x2 = peer_layer(x2, norm_ffn[i], peer_w_q[i], peer_sub_keys[i], u_all, v_all, i)
    return x2.reshape(batch, seq, d)
```

```python
import functools
import math

import jax
import jax.numpy as jnp
from jax import lax
from jax.experimental import pallas as pl
from jax.experimental.pallas import tpu as pltpu

NORM_EPS = 1e-6
N_MIXERS = 2

DA_HEAD_DIM = 128
DA_V_DIM = 2 * DA_HEAD_DIM

RW_HEAD_SIZE = 64
RW_LNX_EPS = RW_HEAD_SIZE * 1e-5
RW_CHUNK = 64

PEER_HEADS = 8
PEER_NKEYS = 128
PEER_HALF = 128
PEER_TOPK = 16

V7X_VMEM_LIMIT_BYTES = 56 * 1024 * 1024
LANES = 128
SUBLANES = 8

_NT = (((1,), (1,)), ((), ()))
_TN = (((0,), (0,)), ((), ()))

_NEG_INF = float("-inf")


def _params(*sem):
    return pltpu.CompilerParams(dimension_semantics=sem,
                                vmem_limit_bytes=V7X_VMEM_LIMIT_BYTES)


def _pick(n, pref):
    if n <= pref:
        return n
    b = pref
    while n % b:
        b //= 2
    return b


def _rmsnorm_kernel(x_ref, g_ref, o_ref):
    x = x_ref[...]
    ms = jnp.mean(x * x, axis=-1, keepdims=True)
    o_ref[...] = (x * lax.rsqrt(ms + NORM_EPS) * g_ref[...]).astype(o_ref.dtype)


def rmsnorm_bf16(x, g):
    n, d = x.shape
    tm = _pick(n, 256)
    return pl.pallas_call(
        _rmsnorm_kernel,
        grid=(n // tm,),
        in_specs=[pl.BlockSpec((tm, d), lambda i: (i, 0)),
                  pl.BlockSpec((1, d), lambda i: (0, 0))],
        out_specs=pl.BlockSpec((tm, d), lambda i: (i, 0)),
        out_shape=jax.ShapeDtypeStruct((n, d), jnp.bfloat16),
        compiler_params=_params("parallel"),
        name="rmsnorm",
    )(x, g.reshape(1, d))


def _mm_kernel(*refs, epilogue, n_norm_tiles, nt):
    a_ref, b_ref = refs[0], refs[1]
    o_ref = refs[-1]
    if nt:
        acc = lax.dot_general(a_ref[...], b_ref[...], _NT, preferred_element_type=jnp.float32)
    else:
        acc = jnp.dot(a_ref[...], b_ref[...], preferred_element_type=jnp.float32)
    if epilogue == "plain":
        o_ref[...] = acc.astype(o_ref.dtype)
    elif epilogue == "residual":
        o_ref[...] = refs[2][...] + acc
    elif epilogue == "tanh":
        o_ref[...] = jnp.tanh(acc).astype(o_ref.dtype)
    elif epilogue == "sigmoid":
        o_ref[...] = jax.nn.sigmoid(acc).astype(o_ref.dtype)
    elif epilogue == "headnorm":
        gain_ref = refs[2]
        j = pl.program_id(1)

        @pl.when(j < n_norm_tiles)
        def _():
            for c in range(acc.shape[1] // DA_HEAD_DIM):
                sl = slice(c * DA_HEAD_DIM, (c + 1) * DA_HEAD_DIM)
                blk = acc[:, sl]
                ms = jnp.mean(blk * blk, axis=-1, keepdims=True)
                o_ref[:, sl] = (blk * lax.rsqrt(ms + NORM_EPS) * gain_ref[:, sl]).astype(o_ref.dtype)

        @pl.when(j >= n_norm_tiles)
        def _():
            o_ref[...] = acc.astype(o_ref.dtype)
    else:
        raise ValueError(epilogue)


def matmul(a, b, *, epilogue="plain", out_dtype=jnp.float32, extra=None,
           n_norm_cols=0, nt=False, n_cols=None, tm=1024, tn=512, name="matmul"):
    m, k = a.shape
    n, k2 = (b.shape if nt else b.shape[::-1])
    assert k == k2
    n = n if n_cols is None else n_cols
    tm = _pick(m, tm)
    tn = _pick(n, tn)
    in_specs = [pl.BlockSpec((tm, k), lambda i, j: (i, 0)),
                pl.BlockSpec((tn, k), lambda i, j: (j, 0)) if nt
                else pl.BlockSpec((k, tn), lambda i, j: (0, j))]
    args = [a, b]
    if epilogue == "residual":
        in_specs.append(pl.BlockSpec((tm, tn), lambda i, j: (i, j)))
        args.append(extra)
    elif epilogue == "headnorm":
        in_specs.append(pl.BlockSpec((1, tn), lambda i, j: (0, j)))
        args.append(extra.reshape(1, n))
        assert n_norm_cols % tn == 0
    kern = functools.partial(_mm_kernel, epilogue=epilogue,
                             n_norm_tiles=n_norm_cols // tn, nt=nt)
    return pl.pallas_call(
        kern,
        grid=(m // tm, n // tn),
        in_specs=in_specs,
        out_specs=pl.BlockSpec((tm, tn), lambda i, j: (i, j)),
        out_shape=jax.ShapeDtypeStruct((m, n), out_dtype),
        compiler_params=_params("parallel", "parallel"),
        name=name,
    )(*args)


ATTN_ROWS = 16


def _attn_kernel(par_ref, zero_ref, q_ref, k_ref, vt_ref, sub_ref, o_ref,
                 m_scr, l_scr, acc_scr, bias_scr, s0_scr, s1_scr, p0_scr, p1_scr,
                 *, tq, tk, tkb, out_scale):
    h = pl.program_id(1)
    qi = pl.program_id(2)
    kj = pl.program_id(3)
    d = DA_HEAD_DIM
    R = ATTN_ROWS
    n_chunk = tk // R
    n_kb = tkb // tk
    slope = par_ref[1 + h]

    @pl.when(kj == 0)
    def _():
        m_scr[...] = jnp.full(m_scr.shape, _NEG_INF, jnp.float32)
        l_scr[...] = jnp.zeros(l_scr.shape, jnp.float32)
        acc_scr[...] = jnp.zeros(acc_scr.shape, jnp.float32)

    @pl.when((kj == 0) & (qi == 0))
    def _():
        for r in range(n_chunk):
            rows = slice(r * R, (r + 1) * R)
            rel = (lax.broadcasted_iota(jnp.int32, (R, tq), 0) + (r * R)
                   - lax.broadcasted_iota(jnp.int32, (R, tq), 1)).astype(jnp.float32)
            bias_scr[0, rows, :] = rel * (-slope)
            bias_scr[1, rows, :] = rel * slope
            bias_scr[2, rows, :] = jnp.abs(rel) * (-slope)

    tiles, shifts = [], []
    for kb in range(n_kb):
        off = kj * tkb + kb * tk - qi * tq
        tiles.append(jnp.where(off > 0, 0, jnp.where(off < 0, 1, 2)))
        shifts.append(jnp.abs(off).astype(jnp.float32) * (-slope))

    z0 = zero_ref[0]
    s_scr = (s0_scr, s1_scr)
    p_scr = (p0_scr, p1_scr)

    def rows_of(ref, start):
        return ref[pl.ds(pl.multiple_of(z0 + start, R), R), :]

    q = q_ref[...]

    def scores(c, kb):
        sv = lax.dot_general(k_ref[kb * tk:(kb + 1) * tk, c * d:(c + 1) * d], q[:, c * d:(c + 1) * d],
                             _NT, preferred_element_type=jnp.float32)
        part = jnp.full((SUBLANES, tq), _NEG_INF, jnp.float32)
        for r in range(n_chunk):
            s = sv[r * R:(r + 1) * R, :] + bias_scr[tiles[kb], r * R:(r + 1) * R, :]
            s_scr[c][kb * tk + r * R:kb * tk + (r + 1) * R, :] = s
            part = jnp.maximum(part, jnp.max(s.reshape(R // SUBLANES, SUBLANES, tq), axis=0))
        return part + shifts[kb]

    blk_maxes = []
    for c in range(2):
        blk_max = scores(c, 0)
        for kb in range(1, n_kb):
            blk_max = jnp.maximum(blk_max, scores(c, kb))
        blk_maxes.append(blk_max)
    for c in range(2):
        m_prev = m_scr[c]
        m_new = jnp.maximum(m_prev, jnp.max(blk_maxes[c], axis=0, keepdims=True))
        alpha = jnp.exp2(m_prev - m_new)
        lsum = jnp.zeros((SUBLANES, tq), jnp.float32)
        for kb in range(n_kb):
            m_kb = m_new - shifts[kb]
            for r in range(n_chunk):
                p = jnp.exp2(rows_of(s_scr[c], kb * tk + r * R) - m_kb)
                p_scr[c][kb * tk + r * R:kb * tk + (r + 1) * R, :] = p.astype(jnp.bfloat16)
                lsum = lsum + jnp.sum(p.reshape(R // SUBLANES, SUBLANES, tq), axis=0)
        l_scr[c] = alpha * l_scr[c] + jnp.sum(lsum, axis=0, keepdims=True)
        acc_scr[c] = acc_scr[c] * alpha + jnp.dot(vt_ref[...], p_scr[c][...],
                                                  preferred_element_type=jnp.float32)
        m_scr[c] = m_new

    @pl.when(kj == pl.num_programs(3) - 1)
    def _():
        lam = par_ref[0]
        ot = acc_scr[0] * (1.0 / l_scr[0]) - acc_scr[1] * (lam / l_scr[1])
        ms = jnp.mean(ot * ot, axis=0, keepdims=True)
        o = (ot * lax.rsqrt(ms + NORM_EPS)).T
        o_ref[...] = (o * (sub_ref[...] * out_scale)).astype(o_ref.dtype)


def diff_attention_core(qk, vt, par, sub_norm, *, batch, seq, n_heads, out_scale,
                        tq=512, tk=512, tkb=2048):
    n = batch * seq
    tq = _pick(seq, tq)
    tkb = _pick(seq, tkb)
    tk = tq
    assert tkb % tk == 0 and tk % ATTN_ROWS == 0
    nq, nk = seq // tq, seq // tkb
    vd = DA_V_DIM
    kern = functools.partial(_attn_kernel, tq=tq, tk=tk, tkb=tkb, out_scale=out_scale)
    return pl.pallas_call(
        kern,
        grid=(batch, n_heads, nq, nk),
        in_specs=[
            pl.BlockSpec(memory_space=pltpu.SMEM),
            pl.BlockSpec(memory_space=pltpu.SMEM),
            pl.BlockSpec((tq, vd), lambda b, h, i, j: (b * nq + i, h)),
            pl.BlockSpec((tkb, vd), lambda b, h, i, j: (b * nk + j, n_heads + h)),
            pl.BlockSpec((vd, tkb), lambda b, h, i, j: (h, b * nk + j)),
            pl.BlockSpec((1, vd), lambda b, h, i, j: (0, 0)),
        ],
        out_specs=pl.BlockSpec((tq, vd), lambda b, h, i, j: (b * nq + i, h)),
        out_shape=jax.ShapeDtypeStruct((n, n_heads * vd), jnp.bfloat16),
        scratch_shapes=[pltpu.VMEM((2, 1, tq), jnp.float32),
                        pltpu.VMEM((2, 1, tq), jnp.float32),
                        pltpu.VMEM((2, vd, tq), jnp.float32),
                        pltpu.VMEM((3, tk, tq), jnp.float32),
                        pltpu.VMEM((tkb, tq), jnp.float32),
                        pltpu.VMEM((tkb, tq), jnp.float32),
                        pltpu.VMEM((tkb, tq), jnp.bfloat16),
                        pltpu.VMEM((tkb, tq), jnp.bfloat16)],
        compiler_params=_params("parallel", "parallel", "arbitrary", "arbitrary"),
        name="diff_attn",
    )(par, jnp.zeros((1,), jnp.int32), qk, qk, vt, sub_norm.reshape(1, vd))


def differential_attention(x2, g_mix, w_qkv, q_norm, k_norm, lam_p, sub_norm, w_o,
                           lambda_init, *, batch, seq):
    n, dm = x2.shape
    d = DA_HEAD_DIM
    n_heads = w_qkv.shape[1] // (3 * 2 * d)
    h = rmsnorm_bf16(x2, g_mix)
    qk_cols = 2 * n_heads * 2 * d
    log2e = math.log2(math.e)
    gains = jnp.concatenate([jnp.tile(q_norm * (d ** -0.5 * log2e), 2 * n_heads),
                             jnp.tile(k_norm, 2 * n_heads)])
    w_bf = w_qkv.astype(jnp.bfloat16)
    qk = matmul(h, w_bf, n_cols=qk_cols, epilogue="headnorm", extra=gains,
                n_norm_cols=qk_cols, out_dtype=jnp.bfloat16, name="attn_qk")
    vt = matmul(w_bf[:, qk_cols:].T, h, nt=True, out_dtype=jnp.bfloat16, name="attn_vt")
    lp = lam_p.astype(jnp.float32)
    lam = jnp.exp(jnp.sum(lp[0] * lp[1])) - jnp.exp(jnp.sum(lp[2] * lp[3])) + lambda_init
    slopes = 2.0 ** (-8.0 * jnp.arange(1, n_heads + 1, dtype=jnp.float32) / n_heads) * log2e
    par = jnp.concatenate([lam.reshape(1), slopes]).astype(jnp.float32)
    o = diff_attention_core(qk, vt, par, sub_norm, batch=batch, seq=seq, n_heads=n_heads,
                            out_scale=1.0 - lambda_init)
    return matmul(o, w_o.astype(jnp.bfloat16), epilogue="residual", extra=x2, name="attn_out")


def _top_values(s, k):
    rows = lax.broadcasted_iota(jnp.int32, s.shape, 0)
    n_rows = s.shape[0]
    vals = []
    for _ in range(k):
        m = jnp.max(s, axis=0, keepdims=True)
        vals.append(m)
        first = jnp.min(jnp.where(s == m, rows, n_rows), axis=0, keepdims=True)
        s = jnp.where(rows == first, _NEG_INF, s)
    return vals


def _compare_exchange(v, i, j):
    hi = jnp.maximum(v[i], v[j])
    lo = jnp.minimum(v[i], v[j])
    v[i], v[j] = hi, lo


def _bitonic_merge_desc(v):
    n = len(v)
    j = n // 2
    while j >= 1:
        for i in range(n):
            if i & j == 0:
                _compare_exchange(v, i, i | j)
        j //= 2


def _sort_desc(v):
    n = len(v)
    k = 2
    while k <= n:
        j = k // 2
        while j >= 1:
            for i in range(n):
                p = i ^ j
                if p > i:
                    if i & k == 0:
                        _compare_exchange(v, i, p)
                    else:
                        _compare_exchange(v, p, i)
            j //= 2
        k *= 2


def _top_values_net(s, k):
    assert s.shape[0] == k * SUBLANES
    v = [s[SUBLANES * i:SUBLANES * (i + 1), :] for i in range(k)]
    _sort_desc(v)
    shift = SUBLANES // 2
    while shift >= 1:
        w = [pltpu.roll(x, shift, axis=0) for x in v]
        v = [jnp.maximum(v[i], w[k - 1 - i]) for i in range(k)]
        _bitonic_merge_desc(v)
        shift //= 2
    return [x[0:1, :] for x in v]


def _peer_route_kernel(q_ref, keys_ref, s1_ref, e1_ref, s2_ref, e2_ref, tau_ref):
    kk = PEER_TOPK
    half = PEER_HALF
    taus = []
    for p in range(PEER_HEADS):
        qa = q_ref[:, (2 * p) * half:(2 * p + 1) * half]
        qb = q_ref[:, (2 * p + 1) * half:(2 * p + 2) * half]
        s1 = lax.dot_general(keys_ref[p, 0], qa, _NT, preferred_element_type=jnp.float32,
                             precision=lax.Precision.HIGHEST)
        s2 = lax.dot_general(keys_ref[p, 1], qb, _NT, preferred_element_type=jnp.float32,
                             precision=lax.Precision.HIGHEST)
        a = _top_values_net(s1, kk)
        b = _top_values_net(s2, kk)
        cands = [a[i] + b[j] for i in range(kk) for j in range(kk) if (i + 1) * (j + 1) <= kk]
        pad = (-len(cands)) % SUBLANES
        cands += [jnp.full_like(a[0], _NEG_INF)] * pad
        top = _top_values(jnp.concatenate(cands, axis=0), kk)
        mx = top[0]
        z = jnp.zeros_like(mx)
        for t in top:
            z = z + jnp.exp(t - mx)
        taus.append(top[kk - 1])
        s1_ref[p] = s1
        s2_ref[p] = s2
        e1_ref[p] = jnp.exp(s1 - a[0])
        e2_ref[p] = jnp.exp(s2 - b[0]) * (0.5 / z)
    tau_ref[...] = jnp.concatenate(taus, axis=0)


def peer_route(q, sub_keys, *, tn=256):
    n = q.shape[0]
    tn = _pick(n, tn)
    ph, nk = PEER_HEADS, PEER_NKEYS
    big = jax.ShapeDtypeStruct((ph, nk, n), jnp.float32)
    bspec = pl.BlockSpec((ph, nk, tn), lambda i: (0, 0, i))
    return pl.pallas_call(
        _peer_route_kernel,
        grid=(n // tn,),
        in_specs=[pl.BlockSpec((tn, q.shape[1]), lambda i: (i, 0)),
                  pl.BlockSpec(sub_keys.shape, lambda i: (0, 0, 0, 0))],
        out_specs=[bspec, bspec, bspec, bspec, pl.BlockSpec((ph, tn), lambda i: (0, i))],
        out_shape=[big, big, big, big, jax.ShapeDtypeStruct((ph, n), jnp.float32)],
        compiler_params=_params("parallel"),
        name="peer_route",
    )(q, sub_keys)


def _peer_dense_kernel(h_ref, u_ref, v_ref, x_ref, s1_ref, e1_ref, s2_ref, e2_ref, tau_ref,
                       o_ref, *, a_per_tile, a_rows):
    e = pl.program_id(1)

    @pl.when(e == 0)
    def _():
        o_ref[...] = x_ref[...]

    act = lax.dot_general(u_ref[...], h_ref[...], _NT, preferred_element_type=jnp.float32)
    act = act * (1.0 + lax.erf(act * (2.0 ** -0.5)))
    nk = PEER_NKEYS
    a_base = lax.rem(e * a_per_tile, a_rows)
    parts = []
    for ai in range(a_per_tile):
        a = a_base + ai
        w = None
        for p in range(PEER_HEADS):
            s1row = s1_ref[p, pl.ds(a, 1), :]
            e1row = e1_ref[p, pl.ds(a, 1), :]
            taurow = tau_ref[pl.ds(p, 1), :]
            sel = jnp.where(s1row + s2_ref[p] >= taurow, e2_ref[p], 0.0) * e1row
            w = sel if w is None else w + sel
        parts.append(act[ai * nk:(ai + 1) * nk, :] * w)
    pt = jnp.concatenate(parts, axis=0)
    o_ref[...] += jnp.dot(pt.T.astype(jnp.bfloat16), v_ref[...], preferred_element_type=jnp.float32)


def peer_dense(hf, u, v, layer, x2, s1, e1, s2, e2, tau, *, tn=512, te=512):
    n, d = hf.shape
    n_exp = u.shape[1]
    tn = _pick(n, tn)
    te = _pick(n_exp, te)
    nk, ph = PEER_NKEYS, PEER_HEADS
    a_per_tile = te // nk
    a_rows = max(a_per_tile, SUBLANES)
    assert a_rows % a_per_tile == 0
    kern = functools.partial(_peer_dense_kernel, a_per_tile=a_per_tile, a_rows=a_rows)
    tok = lambda i, e: (i, 0)
    once = pl.Buffered(1)
    first = pl.BlockSpec((ph, a_rows, tn), lambda i, e: (0, (e * a_per_tile) // a_rows, i))
    second = pl.BlockSpec((ph, nk, tn), lambda i, e: (0, 0, i), pipeline_mode=once)
    return pl.pallas_call(
        kern,
        grid=(n // tn, n_exp // te),
        in_specs=[pl.BlockSpec((tn, d), tok, pipeline_mode=once),
                  pl.BlockSpec((None, te, d), lambda i, e: (layer, e, 0)),
                  pl.BlockSpec((None, te, d), lambda i, e: (layer, e, 0)),
                  pl.BlockSpec((tn, d), tok, pipeline_mode=once),
                  first, first, second, second,
                  pl.BlockSpec((ph, tn), lambda i, e: (0, i))],
        out_specs=pl.BlockSpec((tn, d), tok, pipeline_mode=once),
        out_shape=jax.ShapeDtypeStruct((n, d), jnp.float32),
        compiler_params=_params("parallel", "arbitrary"),
        name="peer_dense",
    )(hf, u, v, x2, s1, e1, s2, e2, tau)


def peer_layer(x2, g_ffn, w_q, sub_keys, u_all, v_all, layer):
    hf = rmsnorm_bf16(x2, g_ffn)
    q = matmul(hf, w_q.astype(jnp.bfloat16), name="peer_q")
    s1, e1, s2, e2, tau = peer_route(q, sub_keys)
    return peer_dense(hf, u_all, v_all, layer, x2, s1, e1, s2, e2, tau)


def _rwkv_mix_kernel(x_ref, xp_ref, xn_ref, g_ref, mu_ref, *o_refs, tm, seq):
    i = pl.program_id(0)
    g = g_ref[...]

    def norm(x):
        ms = jnp.mean(x * x, axis=-1, keepdims=True)
        return x * lax.rsqrt(ms + NORM_EPS) * g

    h = norm(x_ref[...])
    r = lax.broadcasted_iota(jnp.int32, (tm, tm), 0)
    c = lax.broadcasted_iota(jnp.int32, (tm, tm), 1)
    band = jnp.where(r == c, -1.0, jnp.where(jnp.abs(r - c) == 1, 0.5, 0.0)).astype(jnp.bfloat16)
    xx = sum(jnp.dot(band, t, preferred_element_type=jnp.float32) for t in _split3(h))
    t0 = lax.rem(i * tm, seq)
    hp = norm(xp_ref[...])[SUBLANES - 1:SUBLANES, :]
    hn = norm(xn_ref[...])[0:1, :]
    prev_half = jnp.where(t0 == 0, 0.0, 0.5 * hp)
    next_half = jnp.where(t0 + tm == seq, 0.0, 0.5 * hn)
    edge = lax.broadcasted_iota(jnp.int32, (SUBLANES, h.shape[1]), 0)
    first = xx[:SUBLANES, :] + jnp.where(edge == 0, prev_half, 0.0)
    last = xx[tm - SUBLANES:, :] + jnp.where(edge == SUBLANES - 1, next_half, 0.0)
    xx = jnp.concatenate([first, xx[SUBLANES:tm - SUBLANES, :], last], axis=0)
    for idx, o_ref in enumerate(o_refs):
        o_ref[...] = (h + xx * mu_ref[idx:idx + 1, :]).astype(o_ref.dtype)


def rwkv_mix(x2, g_mix, mu, *, seq, tm=256):
    n, d = x2.shape
    tm = _pick(seq, tm)
    assert tm % SUBLANES == 0 and seq % tm == 0
    r8 = tm // SUBLANES
    last = n // SUBLANES - 1
    kern = functools.partial(_rwkv_mix_kernel, tm=tm, seq=seq)
    out = jax.ShapeDtypeStruct((n, d), jnp.bfloat16)
    return pl.pallas_call(
        kern,
        grid=(n // tm,),
        in_specs=[pl.BlockSpec((tm, d), lambda i: (i, 0)),
                  pl.BlockSpec((SUBLANES, d), lambda i: (jnp.maximum(i * r8 - 1, 0), 0)),
                  pl.BlockSpec((SUBLANES, d), lambda i: (jnp.minimum((i + 1) * r8, last), 0)),
                  pl.BlockSpec((1, d), lambda i: (0, 0)),
                  pl.BlockSpec((SUBLANES, d), lambda i: (0, 0))],
        out_specs=[pl.BlockSpec((tm, d), lambda i: (i, 0))] * 6,
        out_shape=[out] * 6,
        compiler_params=_params("parallel"),
        name="rwkv_mix",
    )(x2, x2, x2, g_mix.reshape(1, d), jnp.pad(mu, ((0, SUBLANES - mu.shape[0]), (0, 0))))


HEAD_SUM_WIDTH = 256


def _split3(x):
    hi = x.astype(jnp.bfloat16)
    r1 = x - hi.astype(jnp.float32)
    mid = r1.astype(jnp.bfloat16)
    lo = (r1 - mid.astype(jnp.float32)).astype(jnp.bfloat16)
    return hi, mid, lo


def _head_sum(x, scale):
    w = min(HEAD_SUM_WIDTH, x.shape[1])
    r = lax.broadcasted_iota(jnp.int32, (w, w), 0) // RW_HEAD_SIZE
    c = lax.broadcasted_iota(jnp.int32, (w, w), 1) // RW_HEAD_SIZE
    ones = jnp.where(r == c, scale, 0.0).astype(jnp.bfloat16)
    out = []
    for g in range(x.shape[1] // w):
        terms = _split3(x[:, g * w:(g + 1) * w])
        out.append(sum(jnp.dot(t, ones, preferred_element_type=jnp.float32) for t in terms))
    return out[0] if len(out) == 1 else jnp.concatenate(out, axis=1)


def _rwkv_prep_kernel(k_ref, wl_ref, al_ref, gl_ref, w2_ref, a2_ref, g2_ref, w0_ref, a0_ref,
                      kkw_ref, lw_ref, as_ref, kk_ref, g_ref, *, lora):
    for z in range(2):
        sl = slice(z * lora, (z + 1) * lora)
        wz = w0_ref[z:z + 1, :] + jnp.dot(wl_ref[:, sl], w2_ref[z], preferred_element_type=jnp.float32)
        lw_ref[z] = jax.nn.sigmoid(wz) * (-math.exp(-0.5))
        az = a0_ref[z:z + 1, :] + jnp.dot(al_ref[:, sl], a2_ref[z], preferred_element_type=jnp.float32)
        as_ref[z] = jax.nn.sigmoid(az)
    g_ref[...] = jnp.dot(gl_ref[...], g2_ref[...], preferred_element_type=jnp.float32)
    kr = k_ref[...] * kkw_ref[...]
    ss = _head_sum(kr * kr, 1.0)
    kk_ref[...] = kr * jnp.minimum(lax.rsqrt(ss), 1e12)


def rwkv_prep(k, wl, al, gl, w2, a2, g2, w0, a0, k_k, *, tm=512, tc=512):
    n, d = k.shape
    tm = _pick(n, tm)
    tc = _pick(d, tc)
    lora = w2.shape[1]
    glw = gl.shape[1]
    kern = functools.partial(_rwkv_prep_kernel, lora=lora)
    row = lambda i, j: (i, 0)
    tile = lambda i, j: (i, j)
    two = jax.ShapeDtypeStruct((2, n, d), jnp.float32)
    one = jax.ShapeDtypeStruct((n, d), jnp.float32)
    return pl.pallas_call(
        kern,
        grid=(n // tm, d // tc),
        in_specs=[pl.BlockSpec((tm, tc), tile),
                  pl.BlockSpec((tm, 2 * lora), row),
                  pl.BlockSpec((tm, 2 * lora), row),
                  pl.BlockSpec((tm, glw), row),
                  pl.BlockSpec((2, lora, tc), lambda i, j: (0, 0, j)),
                  pl.BlockSpec((2, lora, tc), lambda i, j: (0, 0, j)),
                  pl.BlockSpec((glw, tc), lambda i, j: (0, j)),
                  pl.BlockSpec((2, tc), lambda i, j: (0, j)),
                  pl.BlockSpec((2, tc), lambda i, j: (0, j)),
                  pl.BlockSpec((1, tc), lambda i, j: (0, j))],
        out_specs=[pl.BlockSpec((2, tm, tc), lambda i, j: (0, i, j)),
                   pl.BlockSpec((2, tm, tc), lambda i, j: (0, i, j)),
                   pl.BlockSpec((tm, tc), tile),
                   pl.BlockSpec((tm, tc), tile)],
        out_shape=[two, two, one, one],
        compiler_params=_params("parallel", "parallel"),
        name="rwkv_prep",
    )(k, wl, al, gl, w2, a2, g2, w0, a0, k_k.reshape(1, d))


def _bdot(a, b):
    return jnp.dot(a.astype(jnp.bfloat16), b.astype(jnp.bfloat16),
                   preferred_element_type=jnp.float32)


def _rwkv_scan_kernel(r_ref, k_ref, v_ref, kk_ref, lw_ref, as_ref, ka_ref, y_ref, s_scr,
                      *, n_chunks, n_heads, n_batch):
    z = pl.program_id(0)
    c = pl.program_id(2)
    L = RW_CHUNK
    N = RW_HEAD_SIZE
    bf = jnp.bfloat16

    @pl.when(c == 0)
    def _():
        s_scr[...] = jnp.zeros(s_scr.shape, jnp.float32)

    sgn = 1 - 2 * z
    row = lax.broadcasted_iota(jnp.int32, (L, 2 * L), 0)
    col = lax.broadcasted_iota(jnp.int32, (L, 2 * L), 1)
    col = jnp.where(col >= L, col - L, col)
    delta = (col - row) * sgn
    strict2 = delta < 0
    incl2 = delta <= 0
    tri = jnp.where(incl2[:, :L], 1.0, 0.0).astype(bf)
    ka = ka_ref[...]
    sls = [slice(j * N, (j + 1) * N) for j in range(n_heads)]
    chains = [(bi, j) for bi in range(n_batch) for j in range(n_heads)]
    ids = range(len(chains))

    def chunk(ci, carry):
        cc = jnp.where(z == 0, ci, n_chunks - 1 - ci)
        rows = pl.ds(pl.multiple_of(cc * L, L), L)
        g_mat, bk, bkp, vj, w_tot = [], [], [], [], []
        for bi in range(n_batch):
            lw = lw_ref[bi, rows, :]
            a_s = as_ref[bi, rows, :]
            kk = kk_ref[bi, rows, :]
            hi, mid, lo = _split3(lw)
            cw = (jnp.dot(tri, hi, preferred_element_type=jnp.float32)
                  + jnp.dot(tri, mid, preferred_element_type=jnp.float32)
                  + jnp.dot(tri, lo, preferred_element_type=jnp.float32))
            tot = jnp.sum(lw, axis=0, keepdims=True)
            w_inv = jnp.exp(-cw)
            w_rem = jnp.exp(tot - cw)
            kd = k_ref[bi, rows, :] * (1.0 + (a_s - 1.0) * ka)
            ba = kk * a_s
            rt = r_ref[bi, rows, :] * jnp.exp(cw)
            at = -kk * jnp.exp(cw - lw)
            bt = ba * w_inv
            kt = kd * w_inv
            bp = ba * w_rem
            kp = kd * w_rem
            v = v_ref[bi, rows, :]
            wt = jnp.exp(tot)
            for sl in sls:
                g_mat.append(jnp.concatenate([at[:, sl], rt[:, sl]], axis=0).astype(bf))
                bk.append(jnp.concatenate([bt[:, sl], kt[:, sl]], axis=0).astype(bf))
                bkp.append(jnp.concatenate([bp[:, sl], kp[:, sl]], axis=0).astype(bf))
                vj.append(v[:, sl].astype(bf))
                w_tot.append(wt[:, sl])
        s_old = [s_scr[bi, j] for bi, j in chains]
        m12 = [lax.dot_general(g_mat[i], jnp.concatenate([bk[i], s_old[i].astype(bf)], axis=0), _NT,
                               preferred_element_type=jnp.float32) for i in ids]
        a_all = [jnp.where(strict2, m12[i][:L, :2 * L], 0.0) for i in ids]
        r_all = [jnp.where(incl2, m12[i][L:, :2 * L], 0.0).astype(bf) for i in ids]
        x0 = [m12[i][:L, 2 * L:] + _bdot(a_all[i][:, L:], vj[i]) for i in ids]
        w = [jnp.concatenate([a_all[i][:, :L], x0[i]], axis=1) for i in ids]
        keep_x = lax.broadcasted_iota(jnp.int32, (L, L + N), 1) >= L
        zpad = jnp.zeros((N, L + N), bf)
        for st in range(L.bit_length() - 1):
            wb = [w[i].astype(bf) for i in ids]
            w = [jnp.dot(wb[i], jnp.concatenate([wb[i], zpad], axis=0),
                         preferred_element_type=jnp.float32) + jnp.where(keep_x, w[i], 0.0)
                 for i in ids]
        uv = [jnp.concatenate([w[i][:, L:].astype(bf), vj[i]], axis=0) for i in ids]
        for i, (bi, j) in enumerate(chains):
            y_ref[bi, rows, sls[j]] = m12[i][L:, 2 * L:] + jnp.dot(r_all[i], uv[i],
                                                                  preferred_element_type=jnp.float32)
        for i, (bi, j) in enumerate(chains):
            s_scr[bi, j] = s_old[i] * w_tot[i] + lax.dot_general(
                uv[i], bkp[i], _TN, preferred_element_type=jnp.float32)
        return carry

    lax.fori_loop(0, n_chunks, chunk, 0)


def rwkv_scan(r, k, v, kk, lw, asig, k_a, *, batch, seq, heads_per_step=16, rows_per_step=256):
    d = r.shape[-1]
    N, L = RW_HEAD_SIZE, RW_CHUNK
    hw = _pick(d, heads_per_step * N)
    tb = _pick(seq, rows_per_step)
    assert tb % L == 0
    nt = seq // tb
    kern = functools.partial(_rwkv_scan_kernel, n_chunks=tb // L, n_heads=hw // N, n_batch=batch)

    def tmap(z, c):
        return c + z * (nt - 1 - 2 * c)

    shared = pl.BlockSpec((batch, tb, hw), lambda z, g, c: (0, tmap(z, c), g))
    per_dir = pl.BlockSpec((None, batch, tb, hw), lambda z, g, c: (z, 0, tmap(z, c), g))
    return pl.pallas_call(
        kern,
        grid=(2, d // hw, nt),
        in_specs=[shared, shared, shared, shared, per_dir, per_dir,
                  pl.BlockSpec((1, hw), lambda z, g, c: (0, g))],
        out_specs=per_dir,
        out_shape=jax.ShapeDtypeStruct((2, batch, seq, d), jnp.float32),
        scratch_shapes=[pltpu.VMEM((batch, hw // N, N, N), jnp.float32)],
        compiler_params=_params("parallel", "parallel", "arbitrary"),
        name="rwkv_scan",
    )(r, k, v, kk, lw, asig, k_a.reshape(1, d))


def _rwkv_post_kernel(y_ref, r_ref, k_ref, v_ref, as_ref, g_ref, lnw_ref, lnb_ref, ka_ref, rk_ref,
                      o_ref):
    y = y_ref[0] + y_ref[1]
    avg = 1.0 / RW_HEAD_SIZE
    mean = _head_sum(y, avg)
    yc = y - mean
    var = _head_sum(yc * yc, avg)
    yn = yc * lax.rsqrt(var + RW_LNX_EPS) * lnw_ref[...] + lnb_ref[...]
    k_sum = k_ref[...] * (2.0 + (as_ref[0] + as_ref[1] - 2.0) * ka_ref[...])
    bonus = _head_sum(r_ref[...] * k_sum * rk_ref[...], 1.0) * v_ref[...]
    o_ref[...] = ((yn + bonus) * g_ref[...]).astype(o_ref.dtype)


def rwkv_post(y, r, k, v, asig, g, ln_w, ln_b, k_a, r_k, *, tm=256, tc=512):
    n, d = r.shape
    tm = _pick(n, tm)
    tc = _pick(d, tc)
    tile = pl.BlockSpec((tm, tc), lambda i, j: (i, j))
    two = pl.BlockSpec((2, tm, tc), lambda i, j: (0, i, j))
    vec = pl.BlockSpec((1, tc), lambda i, j: (0, j))
    return pl.pallas_call(
        _rwkv_post_kernel,
        grid=(n // tm, d // tc),
        in_specs=[two, tile, tile, tile, two, tile, vec, vec, vec, vec],
        out_specs=tile,
        out_shape=jax.ShapeDtypeStruct((n, d), jnp.bfloat16),
        compiler_params=_params("parallel", "parallel"),
        name="rwkv_post",
    )(y, r, k, v, asig, g, ln_w.reshape(1, d), ln_b.reshape(1, d), k_a.reshape(1, d),
      r_k.reshape(1, d))


def rwkv_layer(x2, g_mix, mu, w_rkv, w0, w1, w2, a0, a1, a2, g1, g2, k_k, k_a, r_k, ln_w, ln_b,
               w_o, *, batch, seq):
    n, d = x2.shape
    bf = jnp.bfloat16
    xr, xk, xv, xw, xa, xg = rwkv_mix(x2, g_mix, mu, seq=seq)
    r = matmul(xr, w_rkv[0].astype(bf), name="rwkv_r")
    k = matmul(xk, w_rkv[1].astype(bf), name="rwkv_k")
    v = matmul(xv, w_rkv[2].astype(bf), name="rwkv_v")
    wl = matmul(xw, jnp.concatenate([w1[0], w1[1]], axis=1).astype(bf), epilogue="tanh",
                out_dtype=bf, name="rwkv_w1")
    al = matmul(xa, jnp.concatenate([a1[0], a1[1]], axis=1).astype(bf), out_dtype=bf,
                name="rwkv_a1")
    gpad = (-g1.shape[1]) % LANES
    gl = matmul(xg, jnp.pad(g1, ((0, 0), (0, gpad))).astype(bf), epilogue="sigmoid",
                out_dtype=bf, name="rwkv_g1")
    lw, asig, kk, g = rwkv_prep(k, wl, al, gl, w2.astype(bf), a2.astype(bf),
                                jnp.pad(g2, ((0, gpad), (0, 0))).astype(bf), w0, a0, k_k)
    shp = (batch, seq, d)
    y = rwkv_scan(r.reshape(shp), k.reshape(shp), v.reshape(shp), kk.reshape(shp),
                  lw.reshape((2,) + shp), asig.reshape((2,) + shp), k_a, batch=batch, seq=seq)
    o = rwkv_post(y.reshape(2, n, d), r, k, v, asig, g, ln_w, ln_b, k_a, r_k)
    return matmul(o, w_o.astype(bf), epilogue="residual", extra=x2, name="rwkv_out")


def kernel(x, norm_mix, norm_ffn, attn_w_qkv, attn_q_norm, attn_k_norm, attn_lambda, attn_sub_norm,
           attn_w_o, rwkv_mu, rwkv_w_rkv, rwkv_w0, rwkv_w1, rwkv_w2, rwkv_a0, rwkv_a1, rwkv_a2,
           rwkv_g1, rwkv_g2, rwkv_k_k, rwkv_k_a, rwkv_r_k, rwkv_ln_w, rwkv_ln_b, rwkv_w_o,
           peer_w_q, peer_sub_keys, peer_u, peer_v):
    batch, seq, d = x.shape
    depth = norm_mix.shape[0]
    x2 = x.reshape(batch * seq, d)
    u_all = peer_u.astype(jnp.bfloat16)
    v_all = peer_v.astype(jnp.bfloat16)
    for i in range(depth):
        j = i // N_MIXERS
        if i % N_MIXERS == 0:
            lambda_init = 0.8 - 0.6 * math.exp(-0.3 * i)
            x2 = differential_attention(x2, norm_mix[i], attn_w_qkv[j], attn_q_norm[j],
                                        attn_k_norm[j], attn_lambda[j], attn_sub_norm[j],
                                        attn_w_o[j], lambda_init, batch=batch, seq=seq)
        else:
            x2 = rwkv_layer(x2, norm_mix[i], rwkv_mu[j], rwkv_w_rkv[j], rwkv_w0[j], rwkv_w1[j],
                            rwkv_w2[j], rwkv_a0[j], rwkv_a1[j], rwkv_a2[j], rwkv_g1[j], rwkv_g2[j],
                            rwkv_k_k[j], rwkv_k_a[j], rwkv_r_k[j].reshape(-1), rwkv_ln_w[j],
                            rwkv_ln_b[j], rwkv_w_o[j], batch=batch, seq=seq)
        x2 = peer_layer(x2, norm_ffn[i], peer_w_q[i], peer_sub_keys[i], u_all, v_all, i)
    return x2.reshape(batch, seq, d)
```

```python
import functools
import math

import jax
import jax.numpy as jnp
from jax import lax
from jax.experimental import pallas as pl
from jax.experimental.pallas import tpu as pltpu

NORM_EPS = 1e-6
N_MIXERS = 2

DA_HEAD_DIM = 128
DA_V_DIM = 2 * DA_HEAD_DIM

RW_HEAD_SIZE = 64
RW_LNX_EPS = RW_HEAD_SIZE * 1e-5
RW_CHUNK = 64

PEER_HEADS = 8
PEER_NKEYS = 128
PEER_HALF = 128
PEER_TOPK = 16

V7X_VMEM_LIMIT_BYTES = 56 * 1024 * 1024
LANES = 128
SUBLANES = 8

_NT = (((1,), (1,)), ((), ()))
_TN = (((0,), (0,)), ((), ()))

_NEG_INF = float("-inf")


def _params(*sem):
    return pltpu.CompilerParams(dimension_semantics=sem,
                                vmem_limit_bytes=V7X_VMEM_LIMIT_BYTES)


def _pick(n, pref):
    if n <= pref:
        return n
    b = pref
    while n % b:
        b //= 2
    return b


def _rmsnorm_kernel(x_ref, g_ref, o_ref):
    x = x_ref[...]
    ms = jnp.mean(x * x, axis=-1, keepdims=True)
    o_ref[...] = (x * lax.rsqrt(ms + NORM_EPS) * g_ref[...]).astype(o_ref.dtype)


def rmsnorm_bf16(x, g):
    n, d = x.shape
    tm = _pick(n, 256)
    return pl.pallas_call(
        _rmsnorm_kernel,
        grid=(n // tm,),
        in_specs=[pl.BlockSpec((tm, d), lambda i: (i, 0)),
                  pl.BlockSpec((1, d), lambda i: (0, 0))],
        out_specs=pl.BlockSpec((tm, d), lambda i: (i, 0)),
        out_shape=jax.ShapeDtypeStruct((n, d), jnp.bfloat16),
        compiler_params=_params("parallel"),
        name="rmsnorm",
    )(x, g.reshape(1, d))


def _mm_kernel(*refs, epilogue, n_norm_tiles, nt):
    a_ref, b_ref = refs[0], refs[1]
    o_ref = refs[-1]
    if nt:
        acc = lax.dot_general(a_ref[...], b_ref[...], _NT, preferred_element_type=jnp.float32)
    else:
        acc = jnp.dot(a_ref[...], b_ref[...], preferred_element_type=jnp.float32)
    if epilogue == "plain":
        o_ref[...] = acc.astype(o_ref.dtype)
    elif epilogue == "residual":
        o_ref[...] = refs[2][...] + acc
    elif epilogue == "tanh":
        o_ref[...] = jnp.tanh(acc).astype(o_ref.dtype)
    elif epilogue == "sigmoid":
        o_ref[...] = jax.nn.sigmoid(acc).astype(o_ref.dtype)
    elif epilogue == "headnorm":
        gain_ref = refs[2]
        j = pl.program_id(1)

        @pl.when(j < n_norm_tiles)
        def _():
            for c in range(acc.shape[1] // DA_HEAD_DIM):
                sl = slice(c * DA_HEAD_DIM, (c + 1) * DA_HEAD_DIM)
                blk = acc[:, sl]
                ms = jnp.mean(blk * blk, axis=-1, keepdims=True)
                o_ref[:, sl] = (blk * lax.rsqrt(ms + NORM_EPS) * gain_ref[:, sl]).astype(o_ref.dtype)

        @pl.when(j >= n_norm_tiles)
        def _():
            o_ref[...] = acc.astype(o_ref.dtype)
    else:
        raise ValueError(epilogue)


def matmul(a, b, *, epilogue="plain", out_dtype=jnp.float32, extra=None,
           n_norm_cols=0, nt=False, n_cols=None, tm=1024, tn=512, name="matmul"):
    m, k = a.shape
    n, k2 = (b.shape if nt else b.shape[::-1])
    assert k == k2
    n = n if n_cols is None else n_cols
    tm = _pick(m, tm)
    tn = _pick(n, tn)
    in_specs = [pl.BlockSpec((tm, k), lambda i, j: (i, 0)),
                pl.BlockSpec((tn, k), lambda i, j: (j, 0)) if nt
                else pl.BlockSpec((k, tn), lambda i, j: (0, j))]
    args = [a, b]
    if epilogue == "residual":
        in_specs.append(pl.BlockSpec((tm, tn), lambda i, j: (i, j)))
        args.append(extra)
    elif epilogue == "headnorm":
        in_specs.append(pl.BlockSpec((1, tn), lambda i, j: (0, j)))
        args.append(extra.reshape(1, n))
        assert n_norm_cols % tn == 0
    kern = functools.partial(_mm_kernel, epilogue=epilogue,
                             n_norm_tiles=n_norm_cols // tn, nt=nt)
    return pl.pallas_call(
        kern,
        grid=(m // tm, n // tn),
        in_specs=in_specs,
        out_specs=pl.BlockSpec((tm, tn), lambda i, j: (i, j)),
        out_shape=jax.ShapeDtypeStruct((m, n), out_dtype),
        compiler_params=_params("parallel", "parallel"),
        name=name,
    )(*args)


ATTN_ROWS = 16


def _attn_kernel(par_ref, zero_ref, q_ref, k_ref, vt_ref, sub_ref, o_ref,
                 m_scr, l_scr, acc_scr, bias_scr, s0_scr, s1_scr, p0_scr, p1_scr,
                 *, tq, tk, tkb, out_scale):
    h = pl.program_id(1)
    qi = pl.program_id(2)
    kj = pl.program_id(3)
    d = DA_HEAD_DIM
    R = ATTN_ROWS
    n_chunk = tk // R
    n_kb = tkb // tk
    slope = par_ref[1 + h]

    @pl.when(kj == 0)
    def _():
        m_scr[...] = jnp.full(m_scr.shape, _NEG_INF, jnp.float32)
        l_scr[...] = jnp.zeros(l_scr.shape, jnp.float32)
        acc_scr[...] = jnp.zeros(acc_scr.shape, jnp.float32)

    @pl.when((kj == 0) & (qi == 0))
    def _():
        for r in range(n_chunk):
            rows = slice(r * R, (r + 1) * R)
            rel = (lax.broadcasted_iota(jnp.int32, (R, tq), 0) + (r * R)
                   - lax.broadcasted_iota(jnp.int32, (R, tq), 1)).astype(jnp.float32)
            bias_scr[0, rows, :] = rel * (-slope)
            bias_scr[1, rows, :] = rel * slope
            bias_scr[2, rows, :] = jnp.abs(rel) * (-slope)

    tiles, shifts = [], []
    for kb in range(n_kb):
        off = kj * tkb + kb * tk - qi * tq
        tiles.append(jnp.where(off > 0, 0, jnp.where(off < 0, 1, 2)))
        shifts.append(jnp.abs(off).astype(jnp.float32) * (-slope))

    z0 = zero_ref[0]
    s_scr = (s0_scr, s1_scr)
    p_scr = (p0_scr, p1_scr)

    def rows_of(ref, start):
        return ref[pl.ds(pl.multiple_of(z0 + start, R), R), :]

    q = q_ref[...]

    def scores(c, kb):
        sv = lax.dot_general(k_ref[kb * tk:(kb + 1) * tk, c * d:(c + 1) * d], q[:, c * d:(c + 1) * d],
                             _NT, preferred_element_type=jnp.float32)
        part = jnp.full((SUBLANES, tq), _NEG_INF, jnp.float32)
        for r in range(n_chunk):
            s = sv[r * R:(r + 1) * R, :] + bias_scr[tiles[kb], r * R:(r + 1) * R, :]
            s_scr[c][kb * tk + r * R:kb * tk + (r + 1) * R, :] = s
            part = jnp.maximum(part, jnp.max(s.reshape(R // SUBLANES, SUBLANES, tq), axis=0))
        return part + shifts[kb]

    blk_maxes = []
    for c in range(2):
        blk_max = scores(c, 0)
        for kb in range(1, n_kb):
            blk_max = jnp.maximum(blk_max, scores(c, kb))
        blk_maxes.append(blk_max)
    for c in range(2):
        m_prev = m_scr[c]
        m_new = jnp.maximum(m_prev, jnp.max(blk_maxes[c], axis=0, keepdims=True))
        alpha = jnp.exp2(m_prev - m_new)
        lsum = jnp.zeros((SUBLANES, tq), jnp.float32)
        for kb in range(n_kb):
            m_kb = m_new - shifts[kb]
            for r in range(n_chunk):
                p = jnp.exp2(rows_of(s_scr[c], kb * tk + r * R) - m_kb)
                p_scr[c][kb * tk + r * R:kb * tk + (r + 1) * R, :] = p.astype(jnp.bfloat16)
                lsum = lsum + jnp.sum(p.reshape(R // SUBLANES, SUBLANES, tq), axis=0)
        l_scr[c] = alpha * l_scr[c] + jnp.sum(lsum, axis=0, keepdims=True)
        acc_scr[c] = acc_scr[c] * alpha + jnp.dot(vt_ref[...], p_scr[c][...],
                                                  preferred_element_type=jnp.float32)
        m_scr[c] = m_new

    @pl.when(kj == pl.num_programs(3) - 1)
    def _():
        lam = par_ref[0]
        ot = acc_scr[0] * (1.0 / l_scr[0]) - acc_scr[1] * (lam / l_scr[1])
        ms = jnp.mean(ot * ot, axis=0, keepdims=True)
        o = (ot * lax.rsqrt(ms + NORM_EPS)).T
        o_ref[...] = (o * (sub_ref[...] * out_scale)).astype(o_ref.dtype)


def diff_attention_core(qk, vt, par, sub_norm, *, batch, seq, n_heads, out_scale,
                        tq=256, tk=256, tkb=8192):
    n = batch * seq
    tq = _pick(seq, tq)
    tkb = _pick(seq, tkb)
    tk = tq
    assert tkb % tk == 0 and tk % ATTN_ROWS == 0
    nq, nk = seq // tq, seq // tkb
    vd = DA_V_DIM
    kern = functools.partial(_attn_kernel, tq=tq, tk=tk, tkb=tkb, out_scale=out_scale)
    return pl.pallas_call(
        kern,
        grid=(batch, n_heads, nq, nk),
        in_specs=[
            pl.BlockSpec(memory_space=pltpu.SMEM),
            pl.BlockSpec(memory_space=pltpu.SMEM),
            pl.BlockSpec((tq, vd), lambda b, h, i, j: (b * nq + i, h)),
            pl.BlockSpec((tkb, vd), lambda b, h, i, j: (b * nk + j, n_heads + h)),
            pl.BlockSpec((vd, tkb), lambda b, h, i, j: (h, b * nk + j)),
            pl.BlockSpec((1, vd), lambda b, h, i, j: (0, 0)),
        ],
        out_specs=pl.BlockSpec((tq, vd), lambda b, h, i, j: (b * nq + i, h)),
        out_shape=jax.ShapeDtypeStruct((n, n_heads * vd), jnp.bfloat16),
        scratch_shapes=[pltpu.VMEM((2, 1, tq), jnp.float32),
                        pltpu.VMEM((2, 1, tq), jnp.float32),
                        pltpu.VMEM((2, vd, tq), jnp.float32),
                        pltpu.VMEM((3, tk, tq), jnp.float32),
                        pltpu.VMEM((tkb, tq), jnp.float32),
                        pltpu.VMEM((tkb, tq), jnp.float32),
                        pltpu.VMEM((tkb, tq), jnp.bfloat16),
                        pltpu.VMEM((tkb, tq), jnp.bfloat16)],
        compiler_params=_params("parallel", "parallel", "arbitrary", "arbitrary"),
        name="diff_attn",
    )(par, jnp.zeros((1,), jnp.int32), qk, qk, vt, sub_norm.reshape(1, vd))


def differential_attention(x2, g_mix, w_qkv, q_norm, k_norm, lam_p, sub_norm, w_o,
                           lambda_init, *, batch, seq):
    n, dm = x2.shape
    d = DA_HEAD_DIM
    n_heads = w_qkv.shape[1] // (3 * 2 * d)
    h = rmsnorm_bf16(x2, g_mix)
    qk_cols = 2 * n_heads * 2 * d
    log2e = math.log2(math.e)
    gains = jnp.concatenate([jnp.tile(q_norm * (d ** -0.5 * log2e), 2 * n_heads),
                             jnp.tile(k_norm, 2 * n_heads)])
    w_bf = w_qkv.astype(jnp.bfloat16)
    qk = matmul(h, w_bf, n_cols=qk_cols, epilogue="headnorm", extra=gains,
                n_norm_cols=qk_cols, out_dtype=jnp.bfloat16, name="attn_qk")
    vt = matmul(w_bf[:, qk_cols:].T, h, nt=True, out_dtype=jnp.bfloat16, name="attn_vt")
    lp = lam_p.astype(jnp.float32)
    lam = jnp.exp(jnp.sum(lp[0] * lp[1])) - jnp.exp(jnp.sum(lp[2] * lp[3])) + lambda_init
    slopes = 2.0 ** (-8.0 * jnp.arange(1, n_heads + 1, dtype=jnp.float32) / n_heads) * log2e
    par = jnp.concatenate([lam.reshape(1), slopes]).astype(jnp.float32)
    o = diff_attention_core(qk, vt, par, sub_norm, batch=batch, seq=seq, n_heads=n_heads,
                            out_scale=1.0 - lambda_init)
    return matmul(o, w_o.astype(jnp.bfloat16), epilogue="residual", extra=x2, name="attn_out")


def _top_values(s, k):
    rows = lax.broadcasted_iota(jnp.int32, s.shape, 0)
    n_rows = s.shape[0]
    vals = []
    for _ in range(k):
        m = jnp.max(s, axis=0, keepdims=True)
        vals.append(m)
        first = jnp.min(jnp.where(s == m, rows, n_rows), axis=0, keepdims=True)
        s = jnp.where(rows == first, _NEG_INF, s)
    return vals


def _compare_exchange(v, i, j):
    hi = jnp.maximum(v[i], v[j])
    lo = jnp.minimum(v[i], v[j])
    v[i], v[j] = hi, lo


def _bitonic_merge_desc(v):
    n = len(v)
    j = n // 2
    while j >= 1:
        for i in range(n):
            if i & j == 0:
                _compare_exchange(v, i, i | j)
        j //= 2


def _sort_desc(v):
    n = len(v)
    k = 2
    while k <= n:
        j = k // 2
        while j >= 1:
            for i in range(n):
                p = i ^ j
                if p > i:
                    if i & k == 0:
                        _compare_exchange(v, i, p)
                    else:
                        _compare_exchange(v, p, i)
            j //= 2
        k *= 2


def _top_values_net(s, k):
    assert s.shape[0] == k * SUBLANES
    v = [s[SUBLANES * i:SUBLANES * (i + 1), :] for i in range(k)]
    _sort_desc(v)
    shift = SUBLANES // 2
    while shift >= 1:
        w = [pltpu.roll(x, shift, axis=0) for x in v]
        v = [jnp.maximum(v[i], w[k - 1 - i]) for i in range(k)]
        _bitonic_merge_desc(v)
        shift //= 2
    return [x[0:1, :] for x in v]


def _peer_route_kernel(q_ref, keys_ref, s1_ref, e1_ref, s2_ref, e2_ref, tau_ref):
    kk = PEER_TOPK
    half = PEER_HALF
    taus = []
    for p in range(PEER_HEADS):
        qa = q_ref[:, (2 * p) * half:(2 * p + 1) * half]
        qb = q_ref[:, (2 * p + 1) * half:(2 * p + 2) * half]
        s1 = lax.dot_general(keys_ref[p, 0], qa, _NT, preferred_element_type=jnp.float32,
                             precision=lax.Precision.HIGHEST)
        s2 = lax.dot_general(keys_ref[p, 1], qb, _NT, preferred_element_type=jnp.float32,
                             precision=lax.Precision.HIGHEST)
        a = _top_values_net(s1, kk)
        b = _top_values_net(s2, kk)
        cands = [a[i] + b[j] for i in range(kk) for j in range(kk) if (i + 1) * (j + 1) <= kk]
        pad = (-len(cands)) % SUBLANES
        cands += [jnp.full_like(a[0], _NEG_INF)] * pad
        top = _top_values(jnp.concatenate(cands, axis=0), kk)
        mx = top[0]
        z = jnp.zeros_like(mx)
        for t in top:
            z = z + jnp.exp(t - mx)
        taus.append(top[kk - 1])
        s1_ref[p] = s1
        s2_ref[p] = s2
        e1_ref[p] = jnp.exp(s1 - a[0])
        e2_ref[p] = jnp.exp(s2 - b[0]) * (0.5 / z)
    tau_ref[...] = jnp.concatenate(taus, axis=0)


def peer_route(q, sub_keys, *, tn=256):
    n = q.shape[0]
    tn = _pick(n, tn)
    ph, nk = PEER_HEADS, PEER_NKEYS
    big = jax.ShapeDtypeStruct((ph, nk, n), jnp.float32)
    bspec = pl.BlockSpec((ph, nk, tn), lambda i: (0, 0, i))
    return pl.pallas_call(
        _peer_route_kernel,
        grid=(n // tn,),
        in_specs=[pl.BlockSpec((tn, q.shape[1]), lambda i: (i, 0)),
                  pl.BlockSpec(sub_keys.shape, lambda i: (0, 0, 0, 0))],
        out_specs=[bspec, bspec, bspec, bspec, pl.BlockSpec((ph, tn), lambda i: (0, i))],
        out_shape=[big, big, big, big, jax.ShapeDtypeStruct((ph, n), jnp.float32)],
        compiler_params=_params("parallel"),
        name="peer_route",
    )(q, sub_keys)


def _peer_dense_kernel(h_ref, u_ref, v_ref, x_ref, s1_ref, e1_ref, s2_ref, e2_ref, tau_ref,
                       o_ref, *, a_per_tile, a_rows):
    e = pl.program_id(1)

    @pl.when(e == 0)
    def _():
        o_ref[...] = x_ref[...]

    act = lax.dot_general(u_ref[...], h_ref[...], _NT, preferred_element_type=jnp.float32)
    act = act * (1.0 + lax.erf(act * (2.0 ** -0.5)))
    nk = PEER_NKEYS
    a_base = lax.rem(e * a_per_tile, a_rows)
    parts = []
    for ai in range(a_per_tile):
        a = a_base + ai
        w = None
        for p in range(PEER_HEADS):
            s1row = s1_ref[p, pl.ds(a, 1), :]
            e1row = e1_ref[p, pl.ds(a, 1), :]
            taurow = tau_ref[pl.ds(p, 1), :]
            sel = jnp.where(s1row + s2_ref[p] >= taurow, e2_ref[p], 0.0) * e1row
            w = sel if w is None else w + sel
        parts.append(act[ai * nk:(ai + 1) * nk, :] * w)
    pt = jnp.concatenate(parts, axis=0)
    o_ref[...] += jnp.dot(pt.T.astype(jnp.bfloat16), v_ref[...], preferred_element_type=jnp.float32)


def peer_dense(hf, u, v, layer, x2, s1, e1, s2, e2, tau, *, tn=512, te=512):
    n, d = hf.shape
    n_exp = u.shape[1]
    tn = _pick(n, tn)
    te = _pick(n_exp, te)
    nk, ph = PEER_NKEYS, PEER_HEADS
    a_per_tile = te // nk
    a_rows = max(a_per_tile, SUBLANES)
    assert a_rows % a_per_tile == 0
    kern = functools.partial(_peer_dense_kernel, a_per_tile=a_per_tile, a_rows=a_rows)
    tok = lambda i, e: (i, 0)
    once = pl.Buffered(1)
    first = pl.BlockSpec((ph, a_rows, tn), lambda i, e: (0, (e * a_per_tile) // a_rows, i))
    second = pl.BlockSpec((ph, nk, tn), lambda i, e: (0, 0, i), pipeline_mode=once)
    return pl.pallas_call(
        kern,
        grid=(n // tn, n_exp // te),
        in_specs=[pl.BlockSpec((tn, d), tok, pipeline_mode=once),
                  pl.BlockSpec((None, te, d), lambda i, e: (layer, e, 0)),
                  pl.BlockSpec((None, te, d), lambda i, e: (layer, e, 0)),
                  pl.BlockSpec((tn, d), tok, pipeline_mode=once),
                  first, first, second, second,
                  pl.BlockSpec((ph, tn), lambda i, e: (0, i))],
        out_specs=pl.BlockSpec((tn, d), tok, pipeline_mode=once),
        out_shape=jax.ShapeDtypeStruct((n, d), jnp.float32),
        compiler_params=_params("parallel", "arbitrary"),
        name="peer_dense",
    )(hf, u, v, x2, s1, e1, s2, e2, tau)


def peer_layer(x2, g_ffn, w_q, sub_keys, u_all, v_all, layer):
    hf = rmsnorm_bf16(x2, g_ffn)
    q = matmul(hf, w_q.astype(jnp.bfloat16), name="peer_q")
    s1, e1, s2, e2, tau = peer_route(q, sub_keys)
    return peer_dense(hf, u_all, v_all, layer, x2, s1, e1, s2, e2, tau)


def _rwkv_mix_kernel(x_ref, xp_ref, xn_ref, g_ref, mu_ref, *o_refs, tm, seq):
    i = pl.program_id(0)
    g = g_ref[...]

    def norm(x):
        ms = jnp.mean(x * x, axis=-1, keepdims=True)
        return x * lax.rsqrt(ms + NORM_EPS) * g

    h = norm(x_ref[...])
    r = lax.broadcasted_iota(jnp.int32, (tm, tm), 0)
    c = lax.broadcasted_iota(jnp.int32, (tm, tm), 1)
    band = jnp.where(r == c, -1.0, jnp.where(jnp.abs(r - c) == 1, 0.5, 0.0)).astype(jnp.bfloat16)
    xx = sum(jnp.dot(band, t, preferred_element_type=jnp.float32) for t in _split3(h))
    t0 = lax.rem(i * tm, seq)
    hp = norm(xp_ref[...])[SUBLANES - 1:SUBLANES, :]
    hn = norm(xn_ref[...])[0:1, :]
    prev_half = jnp.where(t0 == 0, 0.0, 0.5 * hp)
    next_half = jnp.where(t0 + tm == seq, 0.0, 0.5 * hn)
    edge = lax.broadcasted_iota(jnp.int32, (SUBLANES, h.shape[1]), 0)
    first = xx[:SUBLANES, :] + jnp.where(edge == 0, prev_half, 0.0)
    last = xx[tm - SUBLANES:, :] + jnp.where(edge == SUBLANES - 1, next_half, 0.0)
    xx = jnp.concatenate([first, xx[SUBLANES:tm - SUBLANES, :], last], axis=0)
    for idx, o_ref in enumerate(o_refs):
        o_ref[...] = (h + xx * mu_ref[idx:idx + 1, :]).astype(o_ref.dtype)


def rwkv_mix(x2, g_mix, mu, *, seq, tm=256):
    n, d = x2.shape
    tm = _pick(seq, tm)
    assert tm % SUBLANES == 0 and seq % tm == 0
    r8 = tm // SUBLANES
    last = n // SUBLANES - 1
    kern = functools.partial(_rwkv_mix_kernel, tm=tm, seq=seq)
    out = jax.ShapeDtypeStruct((n, d), jnp.bfloat16)
    return pl.pallas_call(
        kern,
        grid=(n // tm,),
        in_specs=[pl.BlockSpec((tm, d), lambda i: (i, 0)),
                  pl.BlockSpec((SUBLANES, d), lambda i: (jnp.maximum(i * r8 - 1, 0), 0)),
                  pl.BlockSpec((SUBLANES, d), lambda i: (jnp.minimum((i + 1) * r8, last), 0)),
                  pl.BlockSpec((1, d), lambda i: (0, 0)),
                  pl.BlockSpec((SUBLANES, d), lambda i: (0, 0))],
        out_specs=[pl.BlockSpec((tm, d), lambda i: (i, 0))] * 6,
        out_shape=[out] * 6,
        compiler_params=_params("parallel"),
        name="rwkv_mix",
    )(x2, x2, x2, g_mix.reshape(1, d), jnp.pad(mu, ((0, SUBLANES - mu.shape[0]), (0, 0))))


HEAD_SUM_WIDTH = 256


def _split3(x):
    hi = x.astype(jnp.bfloat16)
    r1 = x - hi.astype(jnp.float32)
    mid = r1.astype(jnp.bfloat16)
    lo = (r1 - mid.astype(jnp.float32)).astype(jnp.bfloat16)
    return hi, mid, lo


def _head_sum(x, scale):
    w = min(HEAD_SUM_WIDTH, x.shape[1])
    r = lax.broadcasted_iota(jnp.int32, (w, w), 0) // RW_HEAD_SIZE
    c = lax.broadcasted_iota(jnp.int32, (w, w), 1) // RW_HEAD_SIZE
    ones = jnp.where(r == c, scale, 0.0).astype(jnp.bfloat16)
    out = []
    for g in range(x.shape[1] // w):
        terms = _split3(x[:, g * w:(g + 1) * w])
        out.append(sum(jnp.dot(t, ones, preferred_element_type=jnp.float32) for t in terms))
    return out[0] if len(out) == 1 else jnp.concatenate(out, axis=1)


def _rwkv_prep_kernel(k_ref, wl_ref, al_ref, gl_ref, w2_ref, a2_ref, g2_ref, w0_ref, a0_ref,
                      kkw_ref, lw_ref, as_ref, kk_ref, g_ref, *, lora):
    for z in range(2):
        sl = slice(z * lora, (z + 1) * lora)
        wz = w0_ref[z:z + 1, :] + jnp.dot(wl_ref[:, sl], w2_ref[z], preferred_element_type=jnp.float32)
        lw_ref[z] = jax.nn.sigmoid(wz) * (-math.exp(-0.5))
        az = a0_ref[z:z + 1, :] + jnp.dot(al_ref[:, sl], a2_ref[z], preferred_element_type=jnp.float32)
        as_ref[z] = jax.nn.sigmoid(az)
    g_ref[...] = jnp.dot(gl_ref[...], g2_ref[...], preferred_element_type=jnp.float32)
    kr = k_ref[...] * kkw_ref[...]
    ss = _head_sum(kr * kr, 1.0)
    kk_ref[...] = kr * jnp.minimum(lax.rsqrt(ss), 1e12)


def rwkv_prep(k, wl, al, gl, w2, a2, g2, w0, a0, k_k, *, tm=512, tc=512):
    n, d = k.shape
    tm = _pick(n, tm)
    tc = _pick(d, tc)
    lora = w2.shape[1]
    glw = gl.shape[1]
    kern = functools.partial(_rwkv_prep_kernel, lora=lora)
    row = lambda i, j: (i, 0)
    tile = lambda i, j: (i, j)
    two = jax.ShapeDtypeStruct((2, n, d), jnp.float32)
    one = jax.ShapeDtypeStruct((n, d), jnp.float32)
    return pl.pallas_call(
        kern,
        grid=(n // tm, d // tc),
        in_specs=[pl.BlockSpec((tm, tc), tile),
                  pl.BlockSpec((tm, 2 * lora), row),
                  pl.BlockSpec((tm, 2 * lora), row),
                  pl.BlockSpec((tm, glw), row),
                  pl.BlockSpec((2, lora, tc), lambda i, j: (0, 0, j)),
                  pl.BlockSpec((2, lora, tc), lambda i, j: (0, 0, j)),
                  pl.BlockSpec((glw, tc), lambda i, j: (0, j)),
                  pl.BlockSpec((2, tc), lambda i, j: (0, j)),
                  pl.BlockSpec((2, tc), lambda i, j: (0, j)),
                  pl.BlockSpec((1, tc), lambda i, j: (0, j))],
        out_specs=[pl.BlockSpec((2, tm, tc), lambda i, j: (0, i, j)),
                   pl.BlockSpec((2, tm, tc), lambda i, j: (0, i, j)),
                   pl.BlockSpec((tm, tc), tile),
                   pl.BlockSpec((tm, tc), tile)],
        out_shape=[two, two, one, one],
        compiler_params=_params("parallel", "parallel"),
        name="rwkv_prep",
    )(k, wl, al, gl, w2, a2, g2, w0, a0, k_k.reshape(1, d))


def _bdot(a, b):
    return jnp.dot(a.astype(jnp.bfloat16), b.astype(jnp.bfloat16),
                   preferred_element_type=jnp.float32)


def _rwkv_scan_kernel(r_ref, k_ref, v_ref, kk_ref, lw_ref, as_ref, ka_ref, y_ref, s_scr,
                      *, n_chunks, n_heads, n_batch):
    z = pl.program_id(0)
    c = pl.program_id(2)
    L = RW_CHUNK
    N = RW_HEAD_SIZE
    bf = jnp.bfloat16

    @pl.when(c == 0)
    def _():
        s_scr[...] = jnp.zeros(s_scr.shape, jnp.float32)

    sgn = 1 - 2 * z
    row = lax.broadcasted_iota(jnp.int32, (L, 2 * L), 0)
    col = lax.broadcasted_iota(jnp.int32, (L, 2 * L), 1)
    col = jnp.where(col >= L, col - L, col)
    delta = (col - row) * sgn
    strict2 = delta < 0
    incl2 = delta <= 0
    tri = jnp.where(incl2[:, :L], 1.0, 0.0).astype(bf)
    ka = ka_ref[...]
    sls = [slice(j * N, (j + 1) * N) for j in range(n_heads)]
    chains = [(bi, j) for bi in range(n_batch) for j in range(n_heads)]
    ids = range(len(chains))

    def chunk(ci, carry):
        cc = jnp.where(z == 0, ci, n_chunks - 1 - ci)
        rows = pl.ds(pl.multiple_of(cc * L, L), L)
        g_mat, bk, bkp, vj, w_tot = [], [], [], [], []
        for bi in range(n_batch):
            lw = lw_ref[bi, rows, :]
            a_s = as_ref[bi, rows, :]
            kk = kk_ref[bi, rows, :]
            hi, mid, lo = _split3(lw)
            cw = (jnp.dot(tri, hi, preferred_element_type=jnp.float32)
                  + jnp.dot(tri, mid, preferred_element_type=jnp.float32)
                  + jnp.dot(tri, lo, preferred_element_type=jnp.float32))
            tot = jnp.sum(lw, axis=0, keepdims=True)
            w_inv = jnp.exp(-cw)
            w_rem = jnp.exp(tot - cw)
            kd = k_ref[bi, rows, :] * (1.0 + (a_s - 1.0) * ka)
            ba = kk * a_s
            rt = r_ref[bi, rows, :] * jnp.exp(cw)
            at = -kk * jnp.exp(cw - lw)
            bt = ba * w_inv
            kt = kd * w_inv
            bp = ba * w_rem
            kp = kd * w_rem
            v = v_ref[bi, rows, :]
            wt = jnp.exp(tot)
            for sl in sls:
                g_mat.append(jnp.concatenate([at[:, sl], rt[:, sl]], axis=0).astype(bf))
                bk.append(jnp.concatenate([bt[:, sl], kt[:, sl]], axis=0).astype(bf))
                bkp.append(jnp.concatenate([bp[:, sl], kp[:, sl]], axis=0).astype(bf))
                vj.append(v[:, sl].astype(bf))
                w_tot.append(wt[:, sl])
        s_old = [s_scr[bi, j] for bi, j in chains]
        m12 = [lax.dot_general(g_mat[i], jnp.concatenate([bk[i], s_old[i].astype(bf)], axis=0), _NT,
                               preferred_element_type=jnp.float32) for i in ids]
        a_all = [jnp.where(strict2, m12[i][:L, :2 * L], 0.0) for i in ids]
        r_all = [jnp.where(incl2, m12[i][L:, :2 * L], 0.0).astype(bf) for i in ids]
        x0 = [m12[i][:L, 2 * L:] + _bdot(a_all[i][:, L:], vj[i]) for i in ids]
        w = [jnp.concatenate([a_all[i][:, :L], x0[i]], axis=1) for i in ids]
        keep_x = lax.broadcasted_iota(jnp.int32, (L, L + N), 1) >= L
        zpad = jnp.zeros((N, L + N), bf)
        for st in range(L.bit_length() - 1):
            wb = [w[i].astype(bf) for i in ids]
            w = [jnp.dot(wb[i], jnp.concatenate([wb[i], zpad], axis=0),
                         preferred_element_type=jnp.float32) + jnp.where(keep_x, w[i], 0.0)
                 for i in ids]
        uv = [jnp.concatenate([w[i][:, L:].astype(bf), vj[i]], axis=0) for i in ids]
        for i, (bi, j) in enumerate(chains):
            y_ref[bi, rows, sls[j]] = m12[i][L:, 2 * L:] + jnp.dot(r_all[i], uv[i],
                                                                  preferred_element_type=jnp.float32)
        for i, (bi, j) in enumerate(chains):
            s_scr[bi, j] = s_old[i] * w_tot[i] + lax.dot_general(
                uv[i], bkp[i], _TN, preferred_element_type=jnp.float32)
        return carry

    lax.fori_loop(0, n_chunks, chunk, 0)


def rwkv_scan(r, k, v, kk, lw, asig, k_a, *, batch, seq, heads_per_step=16, rows_per_step=256):
    d = r.shape[-1]
    N, L = RW_HEAD_SIZE, RW_CHUNK
    hw = _pick(d, heads_per_step * N)
    tb = _pick(seq, rows_per_step)
    assert tb % L == 0
    nt = seq // tb
    kern = functools.partial(_rwkv_scan_kernel, n_chunks=tb // L, n_heads=hw // N, n_batch=batch)

    def tmap(z, c):
        return c + z * (nt - 1 - 2 * c)

    shared = pl.BlockSpec((batch, tb, hw), lambda z, g, c: (0, tmap(z, c), g))
    per_dir = pl.BlockSpec((None, batch, tb, hw), lambda z, g, c: (z, 0, tmap(z, c), g))
    return pl.pallas_call(
        kern,
        grid=(2, d // hw, nt),
        in_specs=[shared, shared, shared, shared, per_dir, per_dir,
                  pl.BlockSpec((1, hw), lambda z, g, c: (0, g))],
        out_specs=per_dir,
        out_shape=jax.ShapeDtypeStruct((2, batch, seq, d), jnp.float32),
        scratch_shapes=[pltpu.VMEM((batch, hw // N, N, N), jnp.float32)],
        compiler_params=_params("parallel", "parallel", "arbitrary"),
        name="rwkv_scan",
    )(r, k, v, kk, lw, asig, k_a.reshape(1, d))


def _rwkv_post_kernel(y_ref, r_ref, k_ref, v_ref, as_ref, g_ref, lnw_ref, lnb_ref, ka_ref, rk_ref,
                      o_ref):
    y = y_ref[0] + y_ref[1]
    avg = 1.0 / RW_HEAD_SIZE
    mean = _head_sum(y, avg)
    yc = y - mean
    var = _head_sum(yc * yc, avg)
    yn = yc * lax.rsqrt(var + RW_LNX_EPS) * lnw_ref[...] + lnb_ref[...]
    k_sum = k_ref[...] * (2.0 + (as_ref[0] + as_ref[1] - 2.0) * ka_ref[...])
    bonus = _head_sum(r_ref[...] * k_sum * rk_ref[...], 1.0) * v_ref[...]
    o_ref[...] = ((yn + bonus) * g_ref[...]).astype(o_ref.dtype)


def rwkv_post(y, r, k, v, asig, g, ln_w, ln_b, k_a, r_k, *, tm=256, tc=512):
    n, d = r.shape
    tm = _pick(n, tm)
    tc = _pick(d, tc)
    tile = pl.BlockSpec((tm, tc), lambda i, j: (i, j))
    two = pl.BlockSpec((2, tm, tc), lambda i, j: (0, i, j))
    vec = pl.BlockSpec((1, tc), lambda i, j: (0, j))
    return pl.pallas_call(
        _rwkv_post_kernel,
        grid=(n // tm, d // tc),
        in_specs=[two, tile, tile, tile, two, tile, vec, vec, vec, vec],
        out_specs=tile,
        out_shape=jax.ShapeDtypeStruct((n, d), jnp.bfloat16),
        compiler_params=_params("parallel", "parallel"),
        name="rwkv_post",
    )(y, r, k, v, asig, g, ln_w.reshape(1, d), ln_b.reshape(1, d), k_a.reshape(1, d),
      r_k.reshape(1, d))


def rwkv_layer(x2, g_mix, mu, w_rkv, w0, w1, w2, a0, a1, a2, g1, g2, k_k, k_a, r_k, ln_w, ln_b,
               w_o, *, batch, seq):
    n, d = x2.shape
    bf = jnp.bfloat16
    xr, xk, xv, xw, xa, xg = rwkv_mix(x2, g_mix, mu, seq=seq)
    r = matmul(xr, w_rkv[0].astype(bf), name="rwkv_r")
    k = matmul(xk, w_rkv[1].astype(bf), name="rwkv_k")
    v = matmul(xv, w_rkv[2].astype(bf), name="rwkv_v")
    wl = matmul(xw, jnp.concatenate([w1[0], w1[1]], axis=1).astype(bf), epilogue="tanh",
                out_dtype=bf, name="rwkv_w1")
    al = matmul(xa, jnp.concatenate([a1[0], a1[1]], axis=1).astype(bf), out_dtype=bf,
                name="rwkv_a1")
    gpad = (-g1.shape[1]) % LANES
    gl = matmul(xg, jnp.pad(g1, ((0, 0), (0, gpad))).astype(bf), epilogue="sigmoid",
                out_dtype=bf, name="rwkv_g1")
    lw, asig, kk, g = rwkv_prep(k, wl, al, gl, w2.astype(bf), a2.astype(bf),
                                jnp.pad(g2, ((0, gpad), (0, 0))).astype(bf), w0, a0, k_k)
    shp = (batch, seq, d)
    y = rwkv_scan(r.reshape(shp), k.reshape(shp), v.reshape(shp), kk.reshape(shp),
                  lw.reshape((2,) + shp), asig.reshape((2,) + shp), k_a, batch=batch, seq=seq)
    o = rwkv_post(y.reshape(2, n, d), r, k, v, asig, g, ln_w, ln_b, k_a, r_k)
    return matmul(o, w_o.astype(bf), epilogue="residual", extra=x2, name="rwkv_out")


def kernel(x, norm_mix, norm_ffn, attn_w_qkv, attn_q_norm, attn_k_norm, attn_lambda, attn_sub_norm,
           attn_w_o, rwkv_mu, rwkv_w_rkv, rwkv_w0, rwkv_w1, rwkv_w2, rwkv_a0, rwkv_a1, rwkv_a2,
           rwkv_g1, rwkv_g2, rwkv_k_k, rwkv_k_a, rwkv_r_k, rwkv_ln_w, rwkv_ln_b, rwkv_w_o,
           peer_w_q, peer_sub_keys, peer_u, peer_v):
    batch, seq, d = x.shape
    depth = norm_mix.shape[0]
    x2 = x.reshape(batch * seq, d)
    u_all = peer_u.astype(jnp.bfloat16)
    v_all = peer_v.astype(jnp.bfloat16)
    for i in range(depth):
        j = i // N_MIXERS
        if i % N_MIXERS == 0:
            lambda_init = 0.8 - 0.6 * math.exp(-0.3 * i)
            x2 = differential_attention(x2, norm_mix[i], attn_w_qkv[j], attn_q_norm[j],
                                        attn_k_norm[j], attn_lambda[j], attn_sub_norm[j],
                                        attn_w_o[j], lambda_init, batch=batch, seq=seq)
        else:
            x2 = rwkv_layer(x2, norm_mix[i], rwkv_mu[j], rwkv_w_rkv[j], rwkv_w0[j], rwkv_w1[j],
                            rwkv_w2[j], rwkv_a0[j], rwkv_a1[j], rwkv_a2[j], rwkv_g1[j], rwkv_g2[j],
                            rwkv_k_k[j], rwkv_k_a[j], rwkv_r_k[j].reshape(-1), rwkv_ln_w[j],
                            rwkv_ln_b[j], rwkv_w_o[j], batch=batch, seq=seq)
        x2 = peer_layer(x2, norm_ffn[i], peer_w_q[i], peer_sub_keys[i], u_all, v_all, i)
    return x2.reshape(batch, seq, d)
```
